```python
import math
import jax, jax.numpy as jnp
from jax import lax
import numpy as np

D_MODEL = 2048
BATCH = 1
SEQ = 8192
DEPTH = 2
DEC_BATCH = 32
DEC_SEQ = 8
PAST_LEN = 8192
PAGE_SIZE = 128

SSM_WIDTH = D_MODEL // 2
SSM_GROUP = 16
SSM_GROUPS = SSM_WIDTH // SSM_GROUP
SSM_STATE = 64
ATTN_WIDTH = D_MODEL // 2
HEAD_DIM = 128
N_HEADS = ATTN_WIDTH // HEAD_DIM
Q_BLOCK = 128
SB_BIAS_INIT = -8.0
N_BRANCHES = 2
IN_COLS = SSM_WIDTH + 3 * ATTN_WIDTH + N_BRANCHES * D_MODEL
MOE_GROUPS = 4
EXPERTS_PER_GROUP = 4
N_EXPERTS = MOE_GROUPS * EXPERTS_PER_GROUP
TOP_K_IN_GROUP = 2
D_EXPERT = D_MODEL // 8
N_MOD = 6
RMS_EPS = 1e-6

kernel_name = 'hybrid_s5_stickbreak_hmoe_step'


def rms_norm(x, g):
    x32 = x.astype(jnp.float32)
    y = x32 * lax.rsqrt(jnp.mean(x32 * x32, axis=-1, keepdims=True) + RMS_EPS)
    return (y * g.astype(jnp.float32)).astype(x.dtype)


def complex_scan_op(e1, e2):
    a1r, a1i, b1r, b1i = e1
    a2r, a2i, b2r, b2i = e2
    return (a2r * a1r - a2i * a1i,
            a2r * a1i + a2i * a1r,
            a2r * b1r - a2i * b1i + b2r,
            a2r * b1i + a2i * b1r + b2i)


def s5_ssm(u, h0_re, h0_im, a_re, a_im, log_dt, b_re, b_im, c_re, c_im, d_skip):
    f32 = jnp.float32
    u = u.astype(f32)
    a_re = a_re.astype(f32)
    a_im = a_im.astype(f32)
    dt = jnp.exp(log_dt.astype(f32))[:, None]
    mag = jnp.exp(dt * a_re)
    ab_re = mag * jnp.cos(dt * a_im)
    ab_im = mag * jnp.sin(dt * a_im)
    den = a_re * a_re + a_im * a_im
    f_re = ((ab_re - 1.0) * a_re + ab_im * a_im) / den
    f_im = (ab_im * a_re - (ab_re - 1.0) * a_im) / den
    b_re = b_re.astype(f32)
    b_im = b_im.astype(f32)
    bb_re = f_re[..., None] * b_re - f_im[..., None] * b_im
    bb_im = f_re[..., None] * b_im + f_im[..., None] * b_re
    bu_re = jnp.einsum('bsgh,gph->bsgp', u, bb_re)
    bu_im = jnp.einsum('bsgh,gph->bsgp', u, bb_im)
    elems = (jnp.broadcast_to(ab_re, bu_re.shape), jnp.broadcast_to(ab_im, bu_im.shape), bu_re, bu_im)
    acc_re, acc_im, h_re, h_im = lax.associative_scan(complex_scan_op, elems, axis=1)
    h0r = h0_re.astype(f32)[:, None]
    h0i = h0_im.astype(f32)[:, None]
    h_re = h_re + acc_re * h0r - acc_im * h0i
    h_im = h_im + acc_re * h0i + acc_im * h0r
    d = d_skip.astype(f32).reshape(u.shape[2], u.shape[3])
    y = (jnp.einsum('bsgp,ghp->bsgh', h_re, c_re.astype(f32))
         - jnp.einsum('bsgp,ghp->bsgh', h_im, c_im.astype(f32))
         + d * u)
    return y, h_re[:, -1], h_im[:, -1]


def stick_breaking_weights(z, q_pos, k_pos):
    mask = k_pos[None, :] < q_pos[:, None]
    log_beta = jax.nn.log_sigmoid(z)
    log_1m = jnp.where(mask, jax.nn.log_sigmoid(-z), 0.0)
    suffix = lax.cumsum(log_1m, axis=z.ndim - 1, reverse=True) - log_1m
    return jnp.where(mask, jnp.exp(log_beta + suffix), 0.0)


def stick_breaking(q, k_parts, v_parts, q_start, bias):
    bsz, sq, nh, dh = q.shape
    qb = min(Q_BLOCK, sq)
    nb = sq // qb
    k_total = sum(k.shape[1] for k in k_parts)
    k_pos = jnp.arange(k_total)
    scale = dh ** -0.5
    sizes = [v.shape[1] for v in v_parts]
    offs = [sum(sizes[:i]) for i in range(len(sizes))]
    bias32 = bias.astype(jnp.float32)[None, :, None, None]

    def block(args):
        i, qblk = args
        z = jnp.concatenate([jnp.einsum('bqhd,bkhd->bhqk', qblk, k).astype(jnp.float32)
                             for k in k_parts], axis=-1) * scale + bias32
        q_pos = q_start + i * qb + jnp.arange(qb)
        w = stick_breaking_weights(z, q_pos, k_pos)
        outs = [jnp.einsum('bhqk,bkhd->bqhd', w[..., o:o + n].astype(v.dtype), v)
                for o, n, v in zip(offs, sizes, v_parts)]
        return sum(outs)

    qs = q.reshape(bsz, nb, qb, nh, dh).transpose(1, 0, 2, 3, 4)
    out = lax.map(block, (jnp.arange(nb), qs))
    return out.transpose(1, 0, 2, 3, 4).reshape(bsz, sq, nh, dh)


def hier_moe(h, w_rg, b_rg, w_re, b_re, w_g, w_u, w_d):
    f32 = jnp.float32
    n = h.shape[0]
    g_logits = (h @ w_rg).astype(f32) + b_rg.astype(f32)
    g_idx = jnp.argmax(g_logits, axis=-1)
    g_prob = jnp.take_along_axis(jax.nn.softmax(g_logits, axis=-1), g_idx[:, None], axis=-1)
    e_logits = ((h @ w_re).astype(f32) + b_re.astype(f32)).reshape(n, MOE_GROUPS, EXPERTS_PER_GROUP)
    e_sel = jnp.take_along_axis(e_logits, g_idx[:, None, None], axis=1)[:, 0]
    top_v, top_i = lax.top_k(e_sel, TOP_K_IN_GROUP)
    w = jax.nn.softmax(top_v, axis=-1) * g_prob
    eid = g_idx[:, None] * EXPERTS_PER_GROUP + top_i
    gates = jnp.sum(jax.nn.one_hot(eid, N_EXPERTS, dtype=f32) * w[..., None], axis=1)
    hg = jnp.einsum('nd,edf->nef', h, w_g)
    hu = jnp.einsum('nd,edf->nef', h, w_u)
    act = jax.nn.silu(hg) * hu * gates.astype(h.dtype)[..., None]
    return jnp.einsum('nef,efd->nd', act, w_d)


def setup_inputs(seed: int = 0) -> dict:
    key = jax.random.key(seed)
    ks = iter(jax.random.split(key, 48))

    def nrm(shape, scale=1.0):
        return jax.random.normal(next(ks), shape, jnp.float32) * scale

    L, D = DEPTH, D_MODEL
    n_pages = PAST_LEN // PAGE_SIZE
    n_used = DEC_BATCH * n_pages
    n_pool = n_used + max(1, n_used // 4)
    x_prompt = nrm((BATCH, SEQ, D))
    x_sample = nrm((DEC_BATCH, DEC_SEQ, D))
    c_prompt = nrm((BATCH, D))
    c_sample = nrm((DEC_BATCH, D))
    cache_k = nrm((L, n_pool, PAGE_SIZE, N_HEADS, HEAD_DIM))
    cache_v = nrm((L, n_pool, PAGE_SIZE, N_HEADS, HEAD_DIM))
    state_ssm_re = nrm((L, DEC_BATCH, SSM_GROUPS, SSM_STATE), 0.1)
    state_ssm_im = nrm((L, DEC_BATCH, SSM_GROUPS, SSM_STATE), 0.1)
    page_table = jax.random.permutation(next(ks), n_pool)[:n_used].reshape(DEC_BATCH, n_pages).astype(jnp.int32)
    w_ada = nrm((L, D, N_MOD * D), D ** -0.5)
    b_ada = nrm((L, N_MOD * D), 0.01)
    norm_mix_g = 1.0 + nrm((L, D), 0.02)
    w_in = nrm((L, D, IN_COLS), D ** -0.5)
    ssm_a_re = -0.5 + nrm((L, SSM_GROUPS, SSM_STATE), 0.01)
    ssm_a_im = jnp.pi * jnp.arange(SSM_STATE, dtype=jnp.float32) + nrm((L, SSM_GROUPS, SSM_STATE), 0.01)
    ssm_log_dt = jax.random.uniform(next(ks), (L, SSM_GROUPS), jnp.float32, math.log(1e-3), math.log(1e-1))
    ssm_b_re = nrm((L, SSM_GROUPS, SSM_STATE, SSM_GROUP), (2 * SSM_GROUP) ** -0.5)
    ssm_b_im = nrm((L, SSM_GROUPS, SSM_STATE, SSM_GROUP), (2 * SSM_GROUP) ** -0.5)
    ssm_c_re = nrm((L, SSM_GROUPS, SSM_GROUP, SSM_STATE), SSM_STATE ** -0.5)
    ssm_c_im = nrm((L, SSM_GROUPS, SSM_GROUP, SSM_STATE), SSM_STATE ** -0.5)
    ssm_d = nrm((L, SSM_WIDTH))
    w_glu = nrm((L, SSM_WIDTH, SSM_WIDTH), SSM_WIDTH ** -0.5)
    b_glu = nrm((L, SSM_WIDTH), 0.01)
    q_norm_g = 1.0 + nrm((L, HEAD_DIM), 0.02)
    k_norm_g = 1.0 + nrm((L, HEAD_DIM), 0.02)
    sb_bias = SB_BIAS_INIT + nrm((L, N_HEADS), 0.1)
    w_branch_ssm = nrm((L, SSM_WIDTH, D), SSM_WIDTH ** -0.5)
    w_branch_attn = nrm((L, ATTN_WIDTH, D), ATTN_WIDTH ** -0.5)
    w_out = nrm((L, D, D), D ** -0.5)
    norm_ffn_g = 1.0 + nrm((L, D), 0.02)
    w_router_group = nrm((L, D, MOE_GROUPS), D ** -0.5)
    b_router_group = nrm((L, MOE_GROUPS), 0.01)
    w_router_expert = nrm((L, D, N_EXPERTS), D ** -0.5)
    b_router_expert = nrm((L, N_EXPERTS), 0.01)
    w_exp_gate = nrm((L, N_EXPERTS, D, D_EXPERT), D ** -0.5)
    w_exp_up = nrm((L, N_EXPERTS, D, D_EXPERT), D ** -0.5)
    w_exp_down = nrm((L, N_EXPERTS, D_EXPERT, D), D_EXPERT ** -0.5)
    return {'x_prompt': x_prompt, 'x_sample': x_sample, 'c_prompt': c_prompt, 'c_sample': c_sample,
            'cache_k': cache_k, 'cache_v': cache_v, 'state_ssm_re': state_ssm_re, 'state_ssm_im': state_ssm_im,
            'page_table': page_table, 'w_ada': w_ada, 'b_ada': b_ada, 'norm_mix_g': norm_mix_g, 'w_in': w_in,
            'ssm_a_re': ssm_a_re, 'ssm_a_im': ssm_a_im, 'ssm_log_dt': ssm_log_dt,
            'ssm_b_re': ssm_b_re, 'ssm_b_im': ssm_b_im, 'ssm_c_re': ssm_c_re, 'ssm_c_im': ssm_c_im,
            'ssm_d': ssm_d, 'w_glu': w_glu, 'b_glu': b_glu, 'q_norm_g': q_norm_g, 'k_norm_g': k_norm_g,
            'sb_bias': sb_bias,
            'w_branch_ssm': w_branch_ssm, 'w_branch_attn': w_branch_attn, 'w_out': w_out,
            'norm_ffn_g': norm_ffn_g, 'w_router_group': w_router_group, 'b_router_group': b_router_group,
            'w_router_expert': w_router_expert, 'b_router_expert': b_router_expert,
            'w_exp_gate': w_exp_gate, 'w_exp_up': w_exp_up, 'w_exp_down': w_exp_down}


def reference(x_prompt, x_sample, c_prompt, c_sample, cache_k, cache_v, state_ssm_re, state_ssm_im,
              page_table, w_ada, b_ada, norm_mix_g, w_in, ssm_a_re, ssm_a_im, ssm_log_dt,
              ssm_b_re, ssm_b_im, ssm_c_re, ssm_c_im, ssm_d, w_glu, b_glu, q_norm_g, k_norm_g,
              sb_bias, w_branch_ssm, w_branch_attn, w_out, norm_ffn_g, w_router_group, b_router_group,
              w_router_expert, b_router_expert, w_exp_gate, w_exp_up, w_exp_down):
    splits = [SSM_WIDTH, SSM_WIDTH + ATTN_WIDTH, SSM_WIDTH + 2 * ATTN_WIDTH, SSM_WIDTH + 3 * ATTN_WIDTH]

    def layer(l, x, c, k_past, v_past, h0_re, h0_im, q_start):
        bsz, s, d = x.shape
        mod = jax.nn.silu(c) @ w_ada[l] + b_ada[l]
        sh1, sc1, g1, sh2, sc2, g2 = jnp.split(mod[:, None, :], N_MOD, axis=-1)
        h = rms_norm(x, norm_mix_g[l]) * (1.0 + sc1) + sh1
        proj = h @ w_in[l]
        u, q, k, v, gate_logits = jnp.split(proj, splits, axis=-1)
        y, hT_re, hT_im = s5_ssm(u.reshape(bsz, s, SSM_GROUPS, SSM_GROUP), h0_re, h0_im,
                                 ssm_a_re[l], ssm_a_im[l], ssm_log_dt[l], ssm_b_re[l], ssm_b_im[l],
                                 ssm_c_re[l], ssm_c_im[l], ssm_d[l])
        y = jax.nn.gelu(y.reshape(bsz, s, SSM_WIDTH).astype(x.dtype))
        a_branch = y * jax.nn.sigmoid(y @ w_glu[l] + b_glu[l])
        q = rms_norm(q.reshape(bsz, s, N_HEADS, HEAD_DIM), q_norm_g[l])
        k = rms_norm(k.reshape(bsz, s, N_HEADS, HEAD_DIM), k_norm_g[l])
        v = v.reshape(bsz, s, N_HEADS, HEAD_DIM)
        if k_past is None:
            k_parts, v_parts = (k,), (v,)
        else:
            k_parts, v_parts = (k_past, k), (v_past, v)
        b_branch = stick_breaking(q, k_parts, v_parts, q_start, sb_bias[l]).reshape(bsz, s, ATTN_WIDTH)
        g_a, g_b = jnp.split(jax.nn.sigmoid(gate_logits), N_BRANCHES, axis=-1)
        merged = g_a * (a_branch @ w_branch_ssm[l]) + g_b * (b_branch @ w_branch_attn[l])
        x = x + g1 * (merged @ w_out[l])
        h2 = rms_norm(x, norm_ffn_g[l]) * (1.0 + sc2) + sh2
        ffn = hier_moe(h2.reshape(bsz * s, d), w_router_group[l], b_router_group[l],
                       w_router_expert[l], b_router_expert[l], w_exp_gate[l], w_exp_up[l], w_exp_down[l])
        x = x + g2 * ffn.reshape(bsz, s, d)
        return x, k, v, hT_re, hT_im

    n_seq, n_pages = page_table.shape
    past_len = n_pages * cache_k.shape[2]
    xp, xs = x_prompt, x_sample
    kp_l, vp_l, rp_l, ip_l, ks_l, vs_l, rs_l, is_l = [], [], [], [], [], [], [], []
    for l in range(DEPTH):
        zeros = jnp.zeros((x_prompt.shape[0], SSM_GROUPS, SSM_STATE), jnp.float32)
        xp, kp, vp, rp, ip = layer(l, xp, c_prompt, None, None, zeros, zeros, 0)
        k_past = cache_k[l][page_table].reshape(n_seq, past_len, N_HEADS, HEAD_DIM)
        v_past = cache_v[l][page_table].reshape(n_seq, past_len, N_HEADS, HEAD_DIM)
        xs, ks, vs, rs, is_ = layer(l, xs, c_sample, k_past, v_past,
                                    state_ssm_re[l], state_ssm_im[l], past_len)
        kp_l.append(kp); vp_l.append(vp); rp_l.append(rp); ip_l.append(ip)
        ks_l.append(ks); vs_l.append(vs); rs_l.append(rs); is_l.append(is_)
    return (xp, xs, jnp.stack(kp_l), jnp.stack(vp_l), jnp.stack(rp_l), jnp.stack(ip_l),
            jnp.stack(ks_l), jnp.stack(vs_l), jnp.stack(rs_l), jnp.stack(is_l))
```

```python
import functools
import math

import jax
import jax.numpy as jnp
from jax import lax
from jax.experimental import pallas as pl
from jax.experimental.pallas import tpu as pltpu

F32 = jnp.float32
BF16 = jnp.bfloat16

RMS_EPS = 1e-6
HEAD_DIM = 128
SSM_GROUP = 16
SSM_STATE = 64
PAIR_CH = 2 * SSM_GROUP
PAIR_ST = 2 * SSM_STATE
MOE_GROUPS = 4
EXPERTS_PER_GROUP = 4
N_EXPERTS = MOE_GROUPS * EXPERTS_PER_GROUP
ROUTER_LANES = 128
N_MOD = 6
ROW_TILE = 256
SUBLANES = 8
VMEM_LIMIT_BYTES = 56 * 1024 * 1024
ATTN_TQ = 256
ATTN_TK = 256


def _params(*sem):
    return pltpu.CompilerParams(dimension_semantics=sem, vmem_limit_bytes=VMEM_LIMIT_BYTES)


def _dot(a, b):
    return jnp.dot(a, b, preferred_element_type=F32)


def _dot_nt(a, b):
    return lax.dot_general(a, b, (((1,), (1,)), ((), ())), preferred_element_type=F32)


def _dot_exact(a, b):
    return jnp.dot(a, b, preferred_element_type=F32, precision=lax.Precision.HIGHEST)


def _adaln_kernel(c_ref, w_ref, b_ref, o_ref):
    c = c_ref[...]
    a = (c * jax.nn.sigmoid(c)).astype(BF16)
    o_ref[...] = _dot(a, w_ref[...].astype(BF16)) + b_ref[...]


def _adaln(c_all, w_ada, b_ada):
    n_layers, d, n_out = w_ada.shape
    r = c_all.shape[0]
    tn = 1024
    return pl.pallas_call(
        _adaln_kernel,
        grid=(n_layers, n_out // tn),
        in_specs=[pl.BlockSpec((r, d), lambda l, j: (0, 0)),
                  pl.BlockSpec((None, d, tn), lambda l, j: (l, 0, j)),
                  pl.BlockSpec((None, 1, tn), lambda l, j: (l, 0, j))],
        out_specs=pl.BlockSpec((None, r, tn), lambda l, j: (l, 0, j)),
        out_shape=jax.ShapeDtypeStruct((n_layers, r, n_out), F32),
        compiler_params=_params("arbitrary", "arbitrary"),
        name="adaln",
    )(c_all, w_ada, b_ada.reshape(n_layers, 1, n_out))


def _rows_mod(v, m):
    tm, d = v.shape
    return v.reshape(tm // SUBLANES, SUBLANES, d), m[:, None, :]


def _modulated_norm(x, g, sc, sh):
    ms = jnp.mean(x * x, axis=-1, keepdims=True)
    y = x * lax.rsqrt(ms + RMS_EPS) * g
    y3, sc3 = _rows_mod(y, sc)
    h = y3 * (1.0 + sc3) + sh[:, None, :]
    return h.reshape(x.shape)


def _mod_spec(n_prompt_tiles, d, which):
    return pl.BlockSpec((None, ROW_TILE // SUBLANES, d),
                        lambda i: (jnp.where(i < n_prompt_tiles, 0, 1), 0, which))


def _norm_mod_kernel(x_ref, g_ref, sc_ref, sh_ref, o_ref):
    o_ref[...] = _modulated_norm(x_ref[...], g_ref[...], sc_ref[...], sh_ref[...]).astype(BF16)


def _norm_mod(x, g, mod3, n_prompt_tiles, sc_idx, sh_idx):
    n, d = x.shape
    return pl.pallas_call(
        _norm_mod_kernel,
        grid=(n // ROW_TILE,),
        in_specs=[pl.BlockSpec((ROW_TILE, d), lambda i: (i, 0)),
                  pl.BlockSpec((1, d), lambda i: (0, 0)),
                  _mod_spec(n_prompt_tiles, d, sc_idx),
                  _mod_spec(n_prompt_tiles, d, sh_idx)],
        out_specs=pl.BlockSpec((ROW_TILE, d), lambda i: (i, 0)),
        out_shape=jax.ShapeDtypeStruct((n, d), BF16),
        compiler_params=_params("arbitrary"),
        name="norm_mod",
    )(x, g.reshape(1, d), mod3, mod3)


def _proj_uv_kernel(h_ref, wu_ref, wv_ref, u_ref, vf_ref, vb_ref):
    h = h_ref[...]
    u_ref[...] = _dot(h, wu_ref[...])
    v = _dot(h, wv_ref[...])
    vf_ref[...] = v
    vb_ref[...] = v.astype(BF16)


def _head_norm(acc, g):
    outs = []
    for hh in range(acc.shape[1] // HEAD_DIM):
        blk = acc[:, hh * HEAD_DIM:(hh + 1) * HEAD_DIM]
        ms = jnp.mean(blk * blk, axis=-1, keepdims=True)
        outs.append(blk * lax.rsqrt(ms + RMS_EPS) * g)
    return jnp.concatenate(outs, axis=1)


def _proj_qk_kernel(h_ref, wq_ref, wk_ref, gq_ref, gk_ref, qb_ref, kf_ref, kb_ref):
    h = h_ref[...]
    qb_ref[...] = _head_norm(_dot(h, wq_ref[...]), gq_ref[...]).astype(BF16)
    k = _head_norm(_dot(h, wk_ref[...]), gk_ref[...])
    kf_ref[...] = k
    kb_ref[...] = k.astype(BF16)


def _proj_gate_kernel(h_ref, w_ref, o_ref):
    o_ref[...] = jax.nn.sigmoid(_dot(h_ref[...], w_ref[...])).astype(BF16)


def _in_proj(h, w_in_b, layer, q_g, k_g, width):
    n, d = h.shape
    nt = n // ROW_TILE
    row = pl.BlockSpec((ROW_TILE, d), lambda i: (i, 0))
    out = pl.BlockSpec((ROW_TILE, width), lambda i: (i, 0))

    def wcol(j):
        return pl.BlockSpec((None, d, width), lambda i: (layer, 0, j))

    sds = lambda dt: jax.ShapeDtypeStruct((n, width), dt)
    u, v_f, v_b = pl.pallas_call(
        _proj_uv_kernel, grid=(nt,),
        in_specs=[row, wcol(0), wcol(3)], out_specs=[out, out, out],
        out_shape=[sds(F32), sds(F32), sds(BF16)],
        compiler_params=_params("arbitrary"), name="proj_uv",
    )(h, w_in_b, w_in_b)
    gspec = pl.BlockSpec((1, HEAD_DIM), lambda i: (0, 0))
    q_b, k_f, k_b = pl.pallas_call(
        _proj_qk_kernel, grid=(nt,),
        in_specs=[row, wcol(1), wcol(2), gspec, gspec], out_specs=[out, out, out],
        out_shape=[sds(BF16), sds(F32), sds(BF16)],
        compiler_params=_params("arbitrary"), name="proj_qk",
    )(h, w_in_b, w_in_b, q_g.reshape(1, HEAD_DIM), k_g.reshape(1, HEAD_DIM))
    n_gate = (w_in_b.shape[2] - 4 * width) // width
    gates = pl.pallas_call(
        _proj_gate_kernel, grid=(n_gate, nt),
        in_specs=[pl.BlockSpec((ROW_TILE, d), lambda j, i: (i, 0)),
                  pl.BlockSpec((None, d, width), lambda j, i: (layer, 0, 4 + j))],
        out_specs=pl.BlockSpec((ROW_TILE, width), lambda j, i: (i, j)),
        out_shape=jax.ShapeDtypeStruct((n, n_gate * width), BF16),
        compiler_params=_params("arbitrary", "arbitrary"), name="proj_gate",
    )(h, w_in_b)
    return u, q_b, k_f, k_b, v_f, v_b, gates


def _cmul(ar, ai, br, bi):
    return ar * br - ai * bi, ar * bi + ai * br


def _discretise(a_re, a_im, log_dt):
    dt = jnp.exp(log_dt)
    mag = jnp.exp(dt * a_re)
    ab_re = mag * jnp.cos(dt * a_im)
    ab_im = mag * jnp.sin(dt * a_im)
    den = a_re * a_re + a_im * a_im
    f_re = ((ab_re - 1.0) * a_re + ab_im * a_im) / den
    f_im = (ab_im * a_re - (ab_re - 1.0) * a_im) / den
    return dt, f_re, f_im


def _abar_pow(a_re, a_im, dt, k):
    mag = jnp.exp(k * dt * a_re)
    th = k * dt * a_im
    return mag * jnp.cos(th), mag * jnp.sin(th)


def _ssm_prep_kernel(as_re_ref, as_im_ref, ldts_ref, al_re_ref, al_im_ref, ldtl_ref,
                     bt_re_ref, bt_im_ref, btr_re_ref, btr_im_ref, c_re_ref, c_im_ref,
                     ct_re_ref, ct_im_ref,
                     m_ref, e_re_ref, e_im_ref, f_re_ref, f_im_ref, p_re_ref, p_im_ref, *, chunk):
    width = chunk * PAIR_CH
    as_re, as_im = as_re_ref[...], as_im_ref[...]
    dts, fs_re, fs_im = _discretise(as_re, as_im, ldts_ref[...])
    al_re, al_im = al_re_ref[...], al_im_ref[...]
    dtl, fl_re, fl_im = _discretise(al_re, al_im, ldtl_ref[...])

    lane_blk = lax.shift_right_logical(lax.broadcasted_iota(jnp.int32, (1, width), 1), 5)
    assert PAIR_CH == 32

    k_rev = (chunk - 1 - lane_blk).astype(F32)
    pk_re, pk_im = _abar_pow(as_re, as_im, dts, k_rev)
    bb_re, bb_im = _cmul(fs_re, fs_im, bt_re_ref[...], bt_im_ref[...])
    w_re, w_im = _cmul(pk_re, pk_im, bb_re, bb_im)
    kr = _dot_exact(c_re_ref[...], w_re) - _dot_exact(c_im_ref[...], w_im)
    lane = lax.broadcasted_iota(jnp.int32, (PAIR_CH, width), 1)
    for t in range(chunk):
        shift = (chunk - 1 - t) * PAIR_CH
        rolled = kr if shift == 0 else pltpu.roll(kr, width - shift, axis=1)
        m_ref[t * PAIR_CH:(t + 1) * PAIR_CH, :] = jnp.where(lane < (t + 1) * PAIR_CH, rolled, 0.0)

    bbt_re, bbt_im = _cmul(fl_re, fl_im, btr_re_ref[...], btr_im_ref[...])
    for tau in range(chunk):
        q_re, q_im = _abar_pow(al_re, al_im, dtl, float(chunk - 1 - tau))
        e_re, e_im = _cmul(q_re, q_im, bbt_re, bbt_im)
        e_re_ref[tau * PAIR_CH:(tau + 1) * PAIR_CH, :] = e_re
        e_im_ref[tau * PAIR_CH:(tau + 1) * PAIR_CH, :] = e_im

    k_f = (lane_blk + 1).astype(F32)
    pf_re, pf_im = _abar_pow(as_re, as_im, dts, k_f)
    ca_re, ca_im = _cmul(ct_re_ref[...], ct_im_ref[...], pf_re, pf_im)
    f_re_ref[...] = ca_re
    f_im_ref[...] = -ca_im

    pt_re, pt_im = _abar_pow(al_re, al_im, dtl, float(chunk))
    p_re_ref[...] = pt_re
    p_im_ref[...] = pt_im


def _ssm_layouts(a_re, a_im, log_dt, b_re, b_im, c_re, c_im, chunk):
    n_groups = a_re.shape[0]
    n_pairs = n_groups // 2
    eye = jnp.eye(2, dtype=F32)

    def sub(x):
        return x.reshape(n_pairs, PAIR_ST, 1)

    def lan(x):
        return x.reshape(n_pairs, 1, PAIR_ST)

    ldt = jnp.broadcast_to(log_dt[:, None], (n_groups, SSM_STATE))

    def b_tiled(b):
        bp = b.reshape(n_pairs, 2, SSM_STATE, SSM_GROUP)
        bd = jnp.einsum('ngph,gk->ngpkh', bp, eye).reshape(n_pairs, PAIR_ST, PAIR_CH)
        return jnp.tile(bd, (1, 1, chunk))

    def b_rows(b):
        bp = b.reshape(n_pairs, 2, SSM_STATE, SSM_GROUP)
        return jnp.einsum('ngph,gk->nghkp', bp, eye).reshape(n_pairs, PAIR_CH, PAIR_ST)

    def c_rows(c):
        cp = c.reshape(n_pairs, 2, SSM_GROUP, SSM_STATE)
        return jnp.einsum('nghp,gk->nghkp', cp, eye).reshape(n_pairs, PAIR_CH, PAIR_ST)

    def c_tiled(c):
        cp = c.reshape(n_pairs, 2, SSM_GROUP, SSM_STATE)
        cd = jnp.einsum('nghp,gk->nkpgh', cp, eye).reshape(n_pairs, PAIR_ST, PAIR_CH)
        return jnp.tile(cd, (1, 1, chunk))

    return (sub(a_re), sub(a_im), sub(ldt), lan(a_re), lan(a_im), lan(ldt),
            b_tiled(b_re), b_tiled(b_im), b_rows(b_re), b_rows(b_im),
            c_rows(c_re), c_rows(c_im), c_tiled(c_re), c_tiled(c_im))


def _ssm_prep(a_re, a_im, log_dt, b_re, b_im, c_re, c_im, chunk):
    n_pairs = a_re.shape[0] // 2
    width = chunk * PAIR_CH
    args = _ssm_layouts(a_re, a_im, log_dt, b_re, b_im, c_re, c_im, chunk)

    def spec(r, c):
        return pl.BlockSpec((None, r, c), lambda i: (i, 0, 0))

    sub, lan = spec(PAIR_ST, 1), spec(1, PAIR_ST)
    wide, rows = spec(PAIR_ST, width), spec(PAIR_CH, PAIR_ST)
    sds = lambda r, c: jax.ShapeDtypeStruct((n_pairs, r, c), F32)
    return pl.pallas_call(
        functools.partial(_ssm_prep_kernel, chunk=chunk),
        grid=(n_pairs,),
        in_specs=[sub, sub, sub, lan, lan, lan, wide, wide, rows, rows, rows, rows, wide, wide],
        out_specs=[spec(width, width), spec(width, PAIR_ST), spec(width, PAIR_ST),
                   wide, wide, lan, lan],
        out_shape=[sds(width, width), sds(width, PAIR_ST), sds(width, PAIR_ST),
                   sds(PAIR_ST, width), sds(PAIR_ST, width), sds(1, PAIR_ST), sds(1, PAIR_ST)],
        compiler_params=_params("arbitrary"),
        name="ssm_prep",
    )(*args)


def _ssm_loc_kernel(u_ref, e_re_ref, e_im_ref, l_re_ref, l_im_ref):
    ub = u_ref[...].astype(BF16)
    l_re_ref[...] = _dot(ub, e_re_ref[...].astype(BF16))
    l_im_ref[...] = _dot(ub, e_im_ref[...].astype(BF16))


def _ssm_loc(u_pairs, e_re, e_im):
    n_pairs, rows, width = u_pairs.shape
    st = pl.BlockSpec((rows, PAIR_ST), lambda i: (0, i))
    sds = jax.ShapeDtypeStruct((rows, n_pairs * PAIR_ST), F32)
    espec = pl.BlockSpec((None, width, PAIR_ST), lambda i: (i, 0, 0))
    return pl.pallas_call(
        _ssm_loc_kernel, grid=(n_pairs,),
        in_specs=[pl.BlockSpec((None, rows, width), lambda i: (i, 0, 0)), espec, espec],
        out_specs=[st, st], out_shape=[sds, sds],
        compiler_params=_params("arbitrary"), name="ssm_loc",
    )(u_pairs, e_re, e_im)


def _ssm_scan_kernel(l_re_ref, l_im_ref, p_re_ref, p_im_ref, s_re_ref, s_im_ref):
    p_re, p_im = p_re_ref[...], p_im_ref[...]
    n_chunks = l_re_ref.shape[0]

    def body(c, s):
        s_re, s_im = s
        s_re_ref[pl.ds(c, 1), :] = s_re
        s_im_ref[pl.ds(c, 1), :] = s_im
        n_re, n_im = _cmul(p_re, p_im, s_re, s_im)
        return n_re + l_re_ref[pl.ds(c, 1), :], n_im + l_im_ref[pl.ds(c, 1), :]

    zero = jnp.zeros_like(p_re)
    lax.fori_loop(0, n_chunks, body, (zero, zero))


def _ssm_scan(loc_re, loc_im, p_re, p_im):
    rows, n_state = loc_re.shape
    tc = 1024
    blk = pl.BlockSpec((rows, tc), lambda j: (0, j))
    par = pl.BlockSpec((1, tc), lambda j: (0, j))
    sds = jax.ShapeDtypeStruct((rows, n_state), F32)
    return pl.pallas_call(
        _ssm_scan_kernel, grid=(n_state // tc,),
        in_specs=[blk, blk, par, par], out_specs=[blk, blk], out_shape=[sds, sds],
        compiler_params=_params("arbitrary"), name="ssm_scan",
    )(loc_re, loc_im, p_re.reshape(1, n_state), p_im.reshape(1, n_state))


def _gelu_tanh(x):
    c = math.sqrt(2.0 / math.pi)
    return 0.5 * x * (1.0 + jnp.tanh(c * (x + 0.044715 * (x * x * x))))


def _ssm_out_kernel(u_ref, s_re_ref, s_im_ref, m_ref, e_re_ref, e_im_ref, f_re_ref, f_im_ref,
                    p_re_ref, p_im_ref, d_ref, y_ref, n_re_ref, n_im_ref):
    u = u_ref[...]
    ub = u.astype(BF16)
    s_re, s_im = s_re_ref[...], s_im_ref[...]
    y = (_dot(ub, m_ref[...].astype(BF16))
         + _dot(s_re.astype(BF16), f_re_ref[...].astype(BF16))
         + _dot(s_im.astype(BF16), f_im_ref[...].astype(BF16))
         + d_ref[...] * u)
    y_ref[...] = _gelu_tanh(y).astype(BF16)
    a_re, a_im = _cmul(p_re_ref[...], p_im_ref[...], s_re, s_im)
    n_re_ref[...] = a_re + _dot(ub, e_re_ref[...].astype(BF16))
    n_im_ref[...] = a_im + _dot(ub, e_im_ref[...].astype(BF16))


def _ssm_out(u_pairs, s_re, s_im, prep, d_tiled):
    n_pairs, rows, width = u_pairs.shape
    m, e_re, e_im, f_re, f_im, p_re, p_im = prep
    st = pl.BlockSpec((rows, PAIR_ST), lambda i: (0, i))
    per = lambda r, c: pl.BlockSpec((None, r, c), lambda i: (i, 0, 0))
    sds = jax.ShapeDtypeStruct((rows, n_pairs * PAIR_ST), F32)
    return pl.pallas_call(
        _ssm_out_kernel, grid=(n_pairs,),
        in_specs=[per(rows, width), st, st, per(width, width), per(width, PAIR_ST), per(width, PAIR_ST),
                  per(PAIR_ST, width), per(PAIR_ST, width), per(1, PAIR_ST), per(1, PAIR_ST), per(1, width)],
        out_specs=[per(rows, width), st, st],
        out_shape=[jax.ShapeDtypeStruct((n_pairs, rows, width), BF16), sds, sds],
        compiler_params=_params("arbitrary"), name="ssm_out",
    )(u_pairs, s_re, s_im, m, e_re, e_im, f_re, f_im, p_re, p_im, d_tiled)


def _to_pairs(u, rows, chunk):
    n_pairs = u.shape[1] // PAIR_CH
    return (u.reshape(rows, chunk, n_pairs, PAIR_CH).transpose(2, 0, 1, 3)
            .reshape(n_pairs, rows, chunk * PAIR_CH))


def _from_pairs(y, rows, chunk):
    n_pairs = y.shape[0]
    return (y.reshape(n_pairs, rows, chunk, PAIR_CH).transpose(1, 2, 0, 3)
            .reshape(rows * chunk, n_pairs * PAIR_CH))


def _glu_kernel(y_ref, w_ref, b_ref, o_ref):
    y = y_ref[...]
    t = _dot(y, w_ref[...]) + b_ref[...]
    o_ref[...] = (y.astype(F32) * jax.nn.sigmoid(t)).astype(BF16)


def _glu(y, w_b, layer, b):
    n, c = y.shape
    return pl.pallas_call(
        _glu_kernel, grid=(n // ROW_TILE,),
        in_specs=[pl.BlockSpec((ROW_TILE, c), lambda i: (i, 0)),
                  pl.BlockSpec((None, c, c), lambda i: (layer, 0, 0)),
                  pl.BlockSpec((1, c), lambda i: (0, 0))],
        out_specs=pl.BlockSpec((ROW_TILE, c), lambda i: (i, 0)),
        out_shape=jax.ShapeDtypeStruct((n, c), BF16),
        compiler_params=_params("arbitrary"), name="glu",
    )(y, w_b, b.reshape(1, c))


def _stick_weights(z, carry, tri, mask):
    sp = jnp.maximum(z, 0.0) + jnp.log1p(jnp.exp(-jnp.abs(z)))
    log_1m = -sp
    if mask is not None:
        log_1m = jnp.where(mask, log_1m, 0.0)
    hi = log_1m.astype(BF16)
    lo = (log_1m - hi.astype(F32)).astype(BF16)
    suffix = _dot(hi, tri) + _dot(lo, tri)
    w = jnp.exp((z - sp) + suffix + carry)
    if mask is not None:
        w = jnp.where(mask, w, 0.0)
    return w, carry + jnp.sum(log_1m, axis=-1, keepdims=True)


def _attn_prompt_kernel(bias_ref, q_ref, k_ref, v_ref, tri_ref, o_ref, *, scale):
    hh = pl.program_id(0)
    i = pl.program_id(1)
    tq, tk = ATTN_TQ, ATTN_TK
    bias = bias_ref[hh]
    q = q_ref[...]
    tri = tri_ref[...]

    def tile(j, carry, acc, mask):
        k = k_ref[pl.ds(pl.multiple_of(j * tk, tk), tk), :]
        v = v_ref[pl.ds(pl.multiple_of(j * tk, tk), tk), :]
        z = _dot_nt(q, k) * scale + bias
        w, carry = _stick_weights(z, carry, tri, mask)
        return carry, acc + _dot(w.astype(BF16), v)

    row = lax.broadcasted_iota(jnp.int32, (tq, tk), 0)
    col = lax.broadcasted_iota(jnp.int32, (tq, tk), 1)
    carry, acc = tile(i, jnp.zeros((tq, 1), F32), jnp.zeros((tq, HEAD_DIM), F32), col < row)

    def body(n, st):
        return tile(i - 1 - n, st[0], st[1], None)

    carry, acc = lax.fori_loop(0, i, body, (carry, acc))
    o_ref[...] = acc.astype(BF16)


def _attn_prompt(q_b, k_b, v_b, bias, seq):
    n_heads = q_b.shape[1] // HEAD_DIM
    tri = (jnp.arange(ATTN_TK)[:, None] > jnp.arange(ATTN_TK)[None, :]).astype(BF16)
    kv = pl.BlockSpec((seq, HEAD_DIM), lambda h, i: (0, h))
    return pl.pallas_call(
        functools.partial(_attn_prompt_kernel, scale=HEAD_DIM ** -0.5),
        grid=(n_heads, seq // ATTN_TQ),
        in_specs=[pl.BlockSpec(memory_space=pltpu.SMEM),
                  pl.BlockSpec((ATTN_TQ, HEAD_DIM), lambda h, i: (i, h)),
                  kv, kv,
                  pl.BlockSpec((ATTN_TK, ATTN_TK), lambda h, i: (0, 0))],
        out_specs=pl.BlockSpec((ATTN_TQ, HEAD_DIM), lambda h, i: (i, h)),
        out_shape=jax.ShapeDtypeStruct((seq, n_heads * HEAD_DIM), BF16),
        compiler_params=_params("arbitrary", "arbitrary"), name="attn_prompt",
    )(bias.astype(F32), q_b, k_b, v_b, tri)


def _attn_sample_kernel(pt_ref, q_ref, kn_ref, vn_ref, kc_ref, vc_ref, bias_ref, tri_ref, o_ref,
                        carry_ref, acc_ref, *, scale, n_heads, n_q):
    p = pl.program_id(1)
    page = kn_ref.shape[0]
    rows = n_heads * n_q
    tri = tri_ref[...]
    bias = bias_ref[...]

    def head_q(hh):
        return q_ref[:, hh * HEAD_DIM:(hh + 1) * HEAD_DIM]

    def sweep(k_of, v_of, mask):
        z = jnp.concatenate([_dot_nt(head_q(hh), k_of(hh)) for hh in range(n_heads)], axis=0)
        w, carry = _stick_weights(z * scale + bias, carry_ref[...], tri, mask)
        carry_ref[...] = carry
        acc_ref[...] += jnp.concatenate(
            [_dot(w[hh * n_q:(hh + 1) * n_q].astype(BF16), v_of(hh)) for hh in range(n_heads)], axis=0)

    @pl.when(p == 0)
    def _():
        carry_ref[...] = jnp.zeros_like(carry_ref)
        acc_ref[...] = jnp.zeros_like(acc_ref)
        q_idx = lax.broadcasted_iota(jnp.int32, (rows, page), 0) & (n_q - 1)
        key = lax.broadcasted_iota(jnp.int32, (rows, page), 1)
        sweep(lambda hh: kn_ref[:, hh * HEAD_DIM:(hh + 1) * HEAD_DIM],
              lambda hh: vn_ref[:, hh * HEAD_DIM:(hh + 1) * HEAD_DIM], key < q_idx)

    sweep(lambda hh: kc_ref[:, hh, :].astype(BF16), lambda hh: vc_ref[:, hh, :].astype(BF16), None)

    @pl.when(p == pl.num_programs(1) - 1)
    def _():
        acc = acc_ref[...]
        for hh in range(n_heads):
            o_ref[:, hh * HEAD_DIM:(hh + 1) * HEAD_DIM] = acc[hh * n_q:(hh + 1) * n_q]


def _attn_sample(q_s, k_new, v_new, cache_k, cache_v, layer, page_table, bias):
    n_seq, n_q, width = q_s.shape
    n_heads = width // HEAD_DIM
    page = cache_k.shape[2]
    n_pages = page_table.shape[1]
    assert n_q & (n_q - 1) == 0
    rows = n_heads * n_q
    tri = (jnp.arange(page)[:, None] > jnp.arange(page)[None, :]).astype(BF16)
    bias_rows = jnp.repeat(bias.astype(F32), n_q).reshape(rows, 1)
    cache = pl.BlockSpec((None, None, page, n_heads, HEAD_DIM),
                         lambda b, p, pt: (layer, pt[b * n_pages + (n_pages - 1 - p)], 0, 0, 0))
    new = pl.BlockSpec((None, page, width), lambda b, p, pt: (b, 0, 0))
    return pl.pallas_call(
        functools.partial(_attn_sample_kernel, scale=HEAD_DIM ** -0.5, n_heads=n_heads, n_q=n_q),
        grid_spec=pltpu.PrefetchScalarGridSpec(
            num_scalar_prefetch=1,
            grid=(n_seq, n_pages),
            in_specs=[pl.BlockSpec((None, n_q, width), lambda b, p, pt: (b, 0, 0)),
                      new, new, cache, cache,
                      pl.BlockSpec((rows, 1), lambda b, p, pt: (0, 0)),
                      pl.BlockSpec((page, page), lambda b, p, pt: (0, 0))],
            out_specs=pl.BlockSpec((None, n_q, width), lambda b, p, pt: (b, 0, 0)),
            scratch_shapes=[pltpu.VMEM((rows, 1), F32), pltpu.VMEM((rows, HEAD_DIM), F32)]),
        out_shape=jax.ShapeDtypeStruct((n_seq, n_q, width), F32),
        compiler_params=_params("arbitrary", "arbitrary"), name="attn_sample",
    )(page_table.reshape(-1), q_s, k_new, v_new, cache_k, cache_v, bias_rows, tri)


def _merge_kernel(a_ref, b_ref, ga_ref, gb_ref, wa_ref, wb_ref, o_ref):
    m = (ga_ref[...].astype(F32) * _dot(a_ref[...], wa_ref[...])
         + gb_ref[...].astype(F32) * _dot(b_ref[...], wb_ref[...]))
    o_ref[...] = m.astype(BF16)


def _merge(a, b, gates, wa_b, wb_b, layer):
    n, c = a.shape
    d = wa_b.shape[2]
    row = pl.BlockSpec((ROW_TILE, c), lambda i: (i, 0))
    w = pl.BlockSpec((None, c, d), lambda i: (layer, 0, 0))
    return pl.pallas_call(
        _merge_kernel, grid=(n // ROW_TILE,),
        in_specs=[row, row,
                  pl.BlockSpec((ROW_TILE, d), lambda i: (i, 0)),
                  pl.BlockSpec((ROW_TILE, d), lambda i: (i, 1)), w, w],
        out_specs=pl.BlockSpec((ROW_TILE, d), lambda i: (i, 0)),
        out_shape=jax.ShapeDtypeStruct((n, d), BF16),
        compiler_params=_params("arbitrary"), name="merge",
    )(a, b, gates, gates, wa_b, wb_b)


def _route(logits):
    lane_i = lax.broadcasted_iota(jnp.int32, logits.shape, 1)
    lane = lane_i.astype(F32)
    neg = jnp.float32(-jnp.inf)
    big = jnp.float32(ROUTER_LANES)
    is_group = (lane_i >= N_EXPERTS) & (lane_i < N_EXPERTS + MOE_GROUPS)
    g_log = jnp.where(is_group, logits, neg)
    g_max = jnp.max(g_log, axis=-1, keepdims=True)
    g_idx = jnp.min(jnp.where(g_log == g_max, lane, big), axis=-1, keepdims=True) - N_EXPERTS
    g_prob = 1.0 / jnp.sum(jnp.where(is_group, jnp.exp(g_log - g_max), 0.0), axis=-1, keepdims=True)
    lane_group = lax.shift_right_logical(lane_i, 2).astype(F32)
    in_group = (lane_i < N_EXPERTS) & (lane_group == g_idx)
    assert EXPERTS_PER_GROUP == 4
    e_log = jnp.where(in_group, logits, neg)
    v1 = jnp.max(e_log, axis=-1, keepdims=True)
    i1 = jnp.min(jnp.where(e_log == v1, lane, big), axis=-1, keepdims=True)
    e_log2 = jnp.where(lane == i1, neg, e_log)
    v2 = jnp.max(e_log2, axis=-1, keepdims=True)
    i2 = jnp.min(jnp.where(e_log2 == v2, lane, big), axis=-1, keepdims=True)
    e2 = jnp.exp(v2 - v1)
    den = 1.0 + e2
    w1 = g_prob / den
    w2 = g_prob * e2 / den
    return jnp.where(lane == i1, w1, jnp.where(lane == i2, w2, 0.0))


def _out_kernel(m_ref, x_ref, w_ref, g1_ref, gn_ref, sc_ref, sh_ref, wr_ref, br_ref,
                x1_ref, h2_ref, gates_ref):
    x = x_ref[...]
    y3, g13 = _rows_mod(_dot(m_ref[...], w_ref[...]), g1_ref[...])
    x1 = x + (y3 * g13).reshape(x.shape)
    x1_ref[...] = x1
    h2 = _modulated_norm(x1, gn_ref[...], sc_ref[...], sh_ref[...]).astype(BF16)
    h2_ref[...] = h2
    gates_ref[...] = _route(_dot(h2, wr_ref[...]) + br_ref[...])


def _out_proj(merged, x, w_out_b, layer, mod3, n_prompt_tiles, norm_g, w_router, b_router):
    n, d = x.shape
    row = pl.BlockSpec((ROW_TILE, d), lambda i: (i, 0))
    small = pl.BlockSpec((ROW_TILE, ROUTER_LANES), lambda i: (i, 0))
    return pl.pallas_call(
        _out_kernel, grid=(n // ROW_TILE,),
        in_specs=[row, row,
                  pl.BlockSpec((None, d, d), lambda i: (layer, 0, 0)),
                  _mod_spec(n_prompt_tiles, d, 2),
                  pl.BlockSpec((1, d), lambda i: (0, 0)),
                  _mod_spec(n_prompt_tiles, d, 4),
                  _mod_spec(n_prompt_tiles, d, 3),
                  pl.BlockSpec((d, ROUTER_LANES), lambda i: (0, 0)),
                  pl.BlockSpec((1, ROUTER_LANES), lambda i: (0, 0))],
        out_specs=[row, row, small],
        out_shape=[jax.ShapeDtypeStruct((n, d), F32), jax.ShapeDtypeStruct((n, d), BF16),
                   jax.ShapeDtypeStruct((n, ROUTER_LANES), F32)],
        compiler_params=_params("arbitrary"), name="out_proj",
    )(merged, x, w_out_b, mod3, norm_g.reshape(1, d), mod3, mod3, w_router, b_router)


def _moe_kernel(h_ref, x_ref, gates_ref, g2_ref, wg_ref, wu_ref, wd_ref, o_ref):
    e = pl.program_id(1)
    h = h_ref[...]
    gates = gates_ref[...]
    lane = lax.broadcasted_iota(jnp.int32, gates.shape, 1)
    gate = jnp.sum(jnp.where(lane == e, gates, 0.0), axis=-1, keepdims=True)
    hg = _dot(h, wg_ref[...])
    hu = _dot(h, wu_ref[...])
    act = (hg * jax.nn.sigmoid(hg)) * hu * gate
    contrib = _dot(act.astype(BF16), wd_ref[...])

    @pl.when(e == 0)
    def _():
        o_ref[...] = contrib

    @pl.when(e > 0)
    def _():
        o_ref[...] += contrib

    @pl.when(e == pl.num_programs(1) - 1)
    def _():
        f3, g23 = _rows_mod(o_ref[...], g2_ref[...])
        o_ref[...] = x_ref[...] + (f3 * g23).reshape(o_ref.shape)


def _moe(h2, x1, gates, mod3, n_prompt_tiles, wg_b, wu_b, wd_b, layer):
    n, d = x1.shape
    f = wg_b.shape[3]
    n_exp = wg_b.shape[1]
    row = pl.BlockSpec((ROW_TILE, d), lambda i, e: (i, 0))
    return pl.pallas_call(
        _moe_kernel, grid=(n // ROW_TILE, n_exp),
        in_specs=[row, row,
                  pl.BlockSpec((ROW_TILE, ROUTER_LANES), lambda i, e: (i, 0)),
                  pl.BlockSpec((None, ROW_TILE // SUBLANES, d),
                               lambda i, e: (jnp.where(i < n_prompt_tiles, 0, 1), 0, 5)),
                  pl.BlockSpec((None, None, d, f), lambda i, e: (layer, e, 0, 0)),
                  pl.BlockSpec((None, None, d, f), lambda i, e: (layer, e, 0, 0)),
                  pl.BlockSpec((None, None, f, d), lambda i, e: (layer, e, 0, 0))],
        out_specs=row,
        out_shape=jax.ShapeDtypeStruct((n, d), F32),
        compiler_params=_params("arbitrary", "arbitrary"), name="moe",
    )(h2, x1, gates, mod3, wg_b, wu_b, wd_b)


def kernel(x_prompt, x_sample, c_prompt, c_sample, cache_k, cache_v, state_ssm_re, state_ssm_im, page_table, w_ada, b_ada, norm_mix_g, w_in, ssm_a_re, ssm_a_im, ssm_log_dt, ssm_b_re, ssm_b_im, ssm_c_re, ssm_c_im, ssm_d, w_glu, b_glu, q_norm_g, k_norm_g, sb_bias, w_branch_ssm, w_branch_attn, w_out, norm_ffn_g, w_router_group, b_router_group, w_router_expert, b_router_expert, w_exp_gate, w_exp_up, w_exp_down):
    n_layers = w_ada.shape[0]
    bsz, seq, d = x_prompt.shape
    n_seq, n_q, _ = x_sample.shape
    assert bsz == 1 and n_q == SUBLANES and n_seq * n_q == ROW_TILE and seq % ROW_TILE == 0
    n_s = n_seq * n_q
    n_prompt_tiles = seq // ROW_TILE
    mod_rows = ROW_TILE // SUBLANES
    ssm_w = ssm_d.shape[1]
    attn_w = w_branch_attn.shape[1]
    assert ssm_w == attn_w
    n_heads = attn_w // HEAD_DIM
    n_groups = ssm_w // SSM_GROUP
    page = cache_k.shape[2]
    chunk_p, chunk_s = 16, n_q
    assert seq % chunk_p == 0

    w_in_b = w_in.astype(BF16)
    w_glu_b = w_glu.astype(BF16)
    w_bs_b = w_branch_ssm.astype(BF16)
    w_ba_b = w_branch_attn.astype(BF16)
    w_out_b = w_out.astype(BF16)
    wg_b, wu_b, wd_b = w_exp_gate.astype(BF16), w_exp_up.astype(BF16), w_exp_down.astype(BF16)

    c_all = jnp.concatenate([c_prompt, c_sample], axis=0)
    pad = -c_all.shape[0] % SUBLANES
    mod = _adaln(jnp.pad(c_all, ((0, pad), (0, 0))), w_ada, b_ada)

    x = jnp.concatenate([x_prompt.reshape(seq, d), x_sample.reshape(n_s, d)], axis=0)
    outs = {k: [] for k in ("kp", "vp", "rp", "ip", "ks", "vs", "rs", "is")}
    for l in range(n_layers):
        mod3 = jnp.stack([jnp.broadcast_to(mod[l, 0], (mod_rows, N_MOD * d)), mod[l, 1:1 + n_seq]])
        h = _norm_mod(x, norm_mix_g[l], mod3, n_prompt_tiles, 1, 0)
        u, q_b, k_f, k_b, v_f, v_b, gates = _in_proj(h, w_in_b, l, q_norm_g[l], k_norm_g[l], ssm_w)

        ssm_par = (ssm_a_re[l], ssm_a_im[l], ssm_log_dt[l], ssm_b_re[l], ssm_b_im[l], ssm_c_re[l], ssm_c_im[l])
        d_pairs = ssm_d[l].reshape(n_groups // 2, 1, PAIR_CH)
        prep_p = _ssm_prep(*ssm_par, chunk_p)
        up = _to_pairs(u[:seq], seq // chunk_p, chunk_p)
        loc_re, loc_im = _ssm_loc(up, prep_p[1], prep_p[2])
        sp_re, sp_im = _ssm_scan(loc_re, loc_im, prep_p[5], prep_p[6])
        yp, np_re, np_im = _ssm_out(up, sp_re, sp_im, prep_p, jnp.tile(d_pairs, (1, 1, chunk_p)))
        prep_s = _ssm_prep(*ssm_par, chunk_s)
        us = _to_pairs(u[seq:], n_seq, chunk_s)
        h0_re = state_ssm_re[l].reshape(n_seq, n_groups * SSM_STATE)
        h0_im = state_ssm_im[l].reshape(n_seq, n_groups * SSM_STATE)
        ys, ns_re, ns_im = _ssm_out(us, h0_re, h0_im, prep_s, jnp.tile(d_pairs, (1, 1, chunk_s)))
        y = jnp.concatenate([_from_pairs(yp, seq // chunk_p, chunk_p), _from_pairs(ys, n_seq, chunk_s)], axis=0)
        a_branch = _glu(y, w_glu_b, l, b_glu[l])

        bp = _attn_prompt(q_b, k_b, v_b, sb_bias[l], seq)
        new_pad = ((0, 0), (0, page - n_q), (0, 0))
        bs = _attn_sample(q_b[seq:].reshape(n_seq, n_q, attn_w),
                          jnp.pad(k_b[seq:].reshape(n_seq, n_q, attn_w), new_pad),
                          jnp.pad(v_b[seq:].reshape(n_seq, n_q, attn_w), new_pad),
                          cache_k, cache_v, l, page_table, sb_bias[l])
        b_branch = jnp.concatenate([bp, bs.reshape(n_s, attn_w).astype(BF16)], axis=0)

        merged = _merge(a_branch, b_branch, gates, w_bs_b, w_ba_b, l)
        w_router = jnp.pad(jnp.concatenate([w_router_expert[l], w_router_group[l]], axis=1),
                           ((0, 0), (0, ROUTER_LANES - N_EXPERTS - MOE_GROUPS))).astype(BF16)
        b_router = jnp.pad(jnp.concatenate([b_router_expert[l], b_router_group[l]]),
                           (0, ROUTER_LANES - N_EXPERTS - MOE_GROUPS)).reshape(1, ROUTER_LANES)
        x1, h2, route = _out_proj(merged, x, w_out_b, l, mod3, n_prompt_tiles, norm_ffn_g[l], w_router, b_router)
        x = _moe(h2, x1, route, mod3, n_prompt_tiles, wg_b, wu_b, wd_b, l)

        outs["kp"].append(k_f[:seq].reshape(1, seq, n_heads, HEAD_DIM))
        outs["vp"].append(v_f[:seq].reshape(1, seq, n_heads, HEAD_DIM))
        outs["rp"].append(np_re[-1].reshape(1, n_groups, SSM_STATE))
        outs["ip"].append(np_im[-1].reshape(1, n_groups, SSM_STATE))
        outs["ks"].append(k_f[seq:].reshape(n_seq, n_q, n_heads, HEAD_DIM))
        outs["vs"].append(v_f[seq:].reshape(n_seq, n_q, n_heads, HEAD_DIM))
        outs["rs"].append(ns_re.reshape(n_seq, n_groups, SSM_STATE))
        outs["is"].append(ns_im.reshape(n_seq, n_groups, SSM_STATE))

    st = lambda k: jnp.stack(outs[k])
    return (x[:seq].reshape(1, seq, d), x[seq:].reshape(n_seq, n_q, d),
            st("kp"), st("vp"), st("rp"), st("ip"), st("ks"), st("vs"), st("rs"), st("is"))
```

```python
import functools
import math

import jax
import jax.numpy as jnp
from jax import lax
from jax.experimental import pallas as pl
from jax.experimental.pallas import tpu as pltpu

F32 = jnp.float32
BF16 = jnp.bfloat16

RMS_EPS = 1e-6
HEAD_DIM = 128
SSM_GROUP = 16
SSM_STATE = 64
PAIR_CH = 2 * SSM_GROUP
PAIR_ST = 2 * SSM_STATE
MOE_GROUPS = 4
EXPERTS_PER_GROUP = 4
N_EXPERTS = MOE_GROUPS * EXPERTS_PER_GROUP
ROUTER_LANES = 128
N_MOD = 6
ROW_TILE = 256
SUBLANES = 8
VMEM_LIMIT_BYTES = 56 * 1024 * 1024
ATTN_TQ = 256
ATTN_TK = 256
PROMPT_TILES_PER_BODY = 4
LOG2_E = 1.4426950408889634
Q_SCALE_LOG2 = HEAD_DIM ** -0.5 * LOG2_E


def _params(*sem):
    return pltpu.CompilerParams(dimension_semantics=sem, vmem_limit_bytes=VMEM_LIMIT_BYTES)


def _dot(a, b):
    return jnp.dot(a, b, preferred_element_type=F32)


def _dot_nt(a, b):
    return lax.dot_general(a, b, (((1,), (1,)), ((), ())), preferred_element_type=F32)


def _dot_exact(a, b):
    return jnp.dot(a, b, preferred_element_type=F32, precision=lax.Precision.HIGHEST)


def _adaln_kernel(c_ref, w_ref, b_ref, o_ref):
    c = c_ref[...]
    a = (c * jax.nn.sigmoid(c)).astype(BF16)
    o_ref[...] = _dot(a, w_ref[...].astype(BF16)) + b_ref[...]


def _adaln(c_all, w_ada, b_ada):
    n_layers, d, n_out = w_ada.shape
    r = c_all.shape[0]
    tn = 1024
    return pl.pallas_call(
        _adaln_kernel,
        grid=(n_layers, n_out // tn),
        in_specs=[pl.BlockSpec((r, d), lambda l, j: (0, 0)),
                  pl.BlockSpec((None, d, tn), lambda l, j: (l, 0, j)),
                  pl.BlockSpec((None, 1, tn), lambda l, j: (l, 0, j))],
        out_specs=pl.BlockSpec((None, r, tn), lambda l, j: (l, 0, j)),
        out_shape=jax.ShapeDtypeStruct((n_layers, r, n_out), F32),
        compiler_params=_params("arbitrary", "arbitrary"),
        name="adaln",
    )(c_all, w_ada, b_ada.reshape(n_layers, 1, n_out))


def _rows_mod(v, m):
    tm, d = v.shape
    return v.reshape(tm // SUBLANES, SUBLANES, d), m[:, None, :]


def _modulated_norm(x, g, sc, sh):
    ms = jnp.mean(x * x, axis=-1, keepdims=True)
    y = x * lax.rsqrt(ms + RMS_EPS) * g
    y3, sc3 = _rows_mod(y, sc)
    h = y3 * (1.0 + sc3) + sh[:, None, :]
    return h.reshape(x.shape)


def _mod_spec(n_prompt_tiles, d, which):
    return pl.BlockSpec((None, ROW_TILE // SUBLANES, d),
                        lambda i: (jnp.where(i < n_prompt_tiles, 0, 1), 0, which))


def _norm_mod_kernel(x_ref, g_ref, sc_ref, sh_ref, o_ref):
    o_ref[...] = _modulated_norm(x_ref[...], g_ref[...], sc_ref[...], sh_ref[...]).astype(BF16)


def _norm_mod(x, g, mod3, n_prompt_tiles, sc_idx, sh_idx):
    n, d = x.shape
    return pl.pallas_call(
        _norm_mod_kernel,
        grid=(n // ROW_TILE,),
        in_specs=[pl.BlockSpec((ROW_TILE, d), lambda i: (i, 0)),
                  pl.BlockSpec((1, d), lambda i: (0, 0)),
                  _mod_spec(n_prompt_tiles, d, sc_idx),
                  _mod_spec(n_prompt_tiles, d, sh_idx)],
        out_specs=pl.BlockSpec((ROW_TILE, d), lambda i: (i, 0)),
        out_shape=jax.ShapeDtypeStruct((n, d), BF16),
        compiler_params=_params("arbitrary"),
        name="norm_mod",
    )(x, g.reshape(1, d), mod3, mod3)


def _store_token_major(ref, val):
    tm = val.shape[0]
    n_heads = val.shape[1] // HEAD_DIM
    for hh in range(n_heads):
        ref[pl.ds(hh, tm, stride=n_heads), :] = val[:, hh * HEAD_DIM:(hh + 1) * HEAD_DIM]


def _proj_uv_kernel(h_ref, wu_ref, wv_ref, u_ref, vt_ref, vb_ref):
    h = h_ref[...]
    u_ref[...] = _dot(h, wu_ref[...])
    v = _dot(h, wv_ref[...])
    _store_token_major(vt_ref, v)
    vb_ref[...] = v.astype(BF16)


def _head_norm(acc, g):
    outs = []
    for hh in range(acc.shape[1] // HEAD_DIM):
        blk = acc[:, hh * HEAD_DIM:(hh + 1) * HEAD_DIM]
        ms = jnp.mean(blk * blk, axis=-1, keepdims=True)
        outs.append(blk * lax.rsqrt(ms + RMS_EPS) * g)
    return jnp.concatenate(outs, axis=1)


def _proj_qk_kernel(h_ref, wq_ref, wk_ref, gq_ref, gk_ref, qb_ref, kt_ref, kb_ref):
    h = h_ref[...]
    qb_ref[...] = (_head_norm(_dot(h, wq_ref[...]), gq_ref[...]) * Q_SCALE_LOG2).astype(BF16)
    k = _head_norm(_dot(h, wk_ref[...]), gk_ref[...])
    _store_token_major(kt_ref, k)
    kb_ref[...] = k.astype(BF16)


def _proj_gate_kernel(h_ref, w_ref, o_ref):
    o_ref[...] = jax.nn.sigmoid(_dot(h_ref[...], w_ref[...]))


def _in_proj(h, w_in_b, layer, q_g, k_g, width):
    n, d = h.shape
    nt = n // ROW_TILE
    row = pl.BlockSpec((ROW_TILE, d), lambda i: (i, 0))
    out = pl.BlockSpec((ROW_TILE, width), lambda i: (i, 0))

    def wcol(j):
        return pl.BlockSpec((None, d, width), lambda i: (layer, 0, j))

    sds = lambda dt: jax.ShapeDtypeStruct((n, width), dt)
    n_heads = width // HEAD_DIM
    tok = pl.BlockSpec((ROW_TILE * n_heads, HEAD_DIM), lambda i: (i, 0))
    tok_sds = jax.ShapeDtypeStruct((n * n_heads, HEAD_DIM), F32)
    u, v_t, v_b = pl.pallas_call(
        _proj_uv_kernel, grid=(nt,),
        in_specs=[row, wcol(0), wcol(3)], out_specs=[out, tok, out],
        out_shape=[sds(F32), tok_sds, sds(BF16)],
        compiler_params=_params("arbitrary"), name="proj_uv",
    )(h, w_in_b, w_in_b)
    gspec = pl.BlockSpec((1, HEAD_DIM), lambda i: (0, 0))
    q_b, k_t, k_b = pl.pallas_call(
        _proj_qk_kernel, grid=(nt,),
        in_specs=[row, wcol(1), wcol(2), gspec, gspec], out_specs=[out, tok, out],
        out_shape=[sds(BF16), tok_sds, sds(BF16)],
        compiler_params=_params("arbitrary"), name="proj_qk",
    )(h, w_in_b, w_in_b, q_g.reshape(1, HEAD_DIM), k_g.reshape(1, HEAD_DIM))
    n_gate = (w_in_b.shape[2] - 4 * width) // width
    gates = pl.pallas_call(
        _proj_gate_kernel, grid=(n_gate, nt),
        in_specs=[pl.BlockSpec((ROW_TILE, d), lambda j, i: (i, 0)),
                  pl.BlockSpec((None, d, width), lambda j, i: (layer, 0, 4 + j))],
        out_specs=pl.BlockSpec((ROW_TILE, width), lambda j, i: (i, j)),
        out_shape=jax.ShapeDtypeStruct((n, n_gate * width), F32),
        compiler_params=_params("arbitrary", "arbitrary"), name="proj_gate",
    )(h, w_in_b)
    return u, q_b, k_t, k_b, v_t, v_b, gates


def _cmul(ar, ai, br, bi):
    return ar * br - ai * bi, ar * bi + ai * br


def _discretise(a_re, a_im, log_dt):
    dt = jnp.exp(log_dt)
    mag = jnp.exp(dt * a_re)
    ab_re = mag * jnp.cos(dt * a_im)
    ab_im = mag * jnp.sin(dt * a_im)
    den = a_re * a_re + a_im * a_im
    f_re = ((ab_re - 1.0) * a_re + ab_im * a_im) / den
    f_im = (ab_im * a_re - (ab_re - 1.0) * a_im) / den
    return dt, f_re, f_im


def _abar_pow(a_re, a_im, dt, k):
    mag = jnp.exp(k * dt * a_re)
    th = k * dt * a_im
    return mag * jnp.cos(th), mag * jnp.sin(th)


def _ssm_prep_kernel(as_re_ref, as_im_ref, ldts_ref, al_re_ref, al_im_ref, ldtl_ref,
                     bt_re_ref, bt_im_ref, btr_re_ref, btr_im_ref, c_re_ref, c_im_ref,
                     ct_re_ref, ct_im_ref,
                     m_ref, e_re_ref, e_im_ref, f_re_ref, f_im_ref, p_re_ref, p_im_ref, *, chunk):
    width = chunk * PAIR_CH
    as_re, as_im = as_re_ref[...], as_im_ref[...]
    dts, fs_re, fs_im = _discretise(as_re, as_im, ldts_ref[...])
    al_re, al_im = al_re_ref[...], al_im_ref[...]
    dtl, fl_re, fl_im = _discretise(al_re, al_im, ldtl_ref[...])

    lane_blk = lax.shift_right_logical(lax.broadcasted_iota(jnp.int32, (1, width), 1), 5)
    assert PAIR_CH == 32

    k_rev = (chunk - 1 - lane_blk).astype(F32)
    pk_re, pk_im = _abar_pow(as_re, as_im, dts, k_rev)
    bb_re, bb_im = _cmul(fs_re, fs_im, bt_re_ref[...], bt_im_ref[...])
    w_re, w_im = _cmul(pk_re, pk_im, bb_re, bb_im)
    kr = _dot_exact(c_re_ref[...], w_re) - _dot_exact(c_im_ref[...], w_im)
    lane = lax.broadcasted_iota(jnp.int32, (PAIR_CH, width), 1)
    for t in range(chunk):
        shift = (chunk - 1 - t) * PAIR_CH
        rolled = kr if shift == 0 else pltpu.roll(kr, width - shift, axis=1)
        m_ref[t * PAIR_CH:(t + 1) * PAIR_CH, :] = jnp.where(lane < (t + 1) * PAIR_CH, rolled, 0.0)

    bbt_re, bbt_im = _cmul(fl_re, fl_im, btr_re_ref[...], btr_im_ref[...])
    for tau in range(chunk):
        q_re, q_im = _abar_pow(al_re, al_im, dtl, float(chunk - 1 - tau))
        e_re, e_im = _cmul(q_re, q_im, bbt_re, bbt_im)
        e_re_ref[tau * PAIR_CH:(tau + 1) * PAIR_CH, :] = e_re
        e_im_ref[tau * PAIR_CH:(tau + 1) * PAIR_CH, :] = e_im

    k_f = (lane_blk + 1).astype(F32)
    pf_re, pf_im = _abar_pow(as_re, as_im, dts, k_f)
    ca_re, ca_im = _cmul(ct_re_ref[...], ct_im_ref[...], pf_re, pf_im)
    f_re_ref[...] = ca_re
    f_im_ref[...] = -ca_im

    pt_re, pt_im = _abar_pow(al_re, al_im, dtl, float(chunk))
    ph_re, ph_im = _abar_pow(al_re, al_im, dtl, float(chunk // 2))
    p_re_ref[...] = jnp.concatenate([pt_re, ph_re], axis=0)
    p_im_ref[...] = jnp.concatenate([pt_im, ph_im], axis=0)


def _ssm_layouts(a_re, a_im, log_dt, b_re, b_im, c_re, c_im, chunk):
    n_groups = a_re.shape[0]
    n_pairs = n_groups // 2
    eye = jnp.eye(2, dtype=F32)

    def sub(x):
        return x.reshape(n_pairs, PAIR_ST, 1)

    def lan(x):
        return x.reshape(n_pairs, 1, PAIR_ST)

    ldt = jnp.broadcast_to(log_dt[:, None], (n_groups, SSM_STATE))

    def b_tiled(b):
        bp = b.reshape(n_pairs, 2, SSM_STATE, SSM_GROUP)
        bd = jnp.einsum('ngph,gk->ngpkh', bp, eye).reshape(n_pairs, PAIR_ST, PAIR_CH)
        return jnp.tile(bd, (1, 1, chunk))

    def b_rows(b):
        bp = b.reshape(n_pairs, 2, SSM_STATE, SSM_GROUP)
        return jnp.einsum('ngph,gk->nghkp', bp, eye).reshape(n_pairs, PAIR_CH, PAIR_ST)

    def c_rows(c):
        cp = c.reshape(n_pairs, 2, SSM_GROUP, SSM_STATE)
        return jnp.einsum('nghp,gk->nghkp', cp, eye).reshape(n_pairs, PAIR_CH, PAIR_ST)

    def c_tiled(c):
        cp = c.reshape(n_pairs, 2, SSM_GROUP, SSM_STATE)
        cd = jnp.einsum('nghp,gk->nkpgh', cp, eye).reshape(n_pairs, PAIR_ST, PAIR_CH)
        return jnp.tile(cd, (1, 1, chunk))

    return (sub(a_re), sub(a_im), sub(ldt), lan(a_re), lan(a_im), lan(ldt),
            b_tiled(b_re), b_tiled(b_im), b_rows(b_re), b_rows(b_im),
            c_rows(c_re), c_rows(c_im), c_tiled(c_re), c_tiled(c_im))


def _ssm_prep(a_re, a_im, log_dt, b_re, b_im, c_re, c_im, chunk):
    n_pairs = a_re.shape[0] // 2
    width = chunk * PAIR_CH
    args = _ssm_layouts(a_re, a_im, log_dt, b_re, b_im, c_re, c_im, chunk)

    def spec(r, c):
        return pl.BlockSpec((None, r, c), lambda i: (i, 0, 0))

    sub, lan = spec(PAIR_ST, 1), spec(1, PAIR_ST)
    wide, rows = spec(PAIR_ST, width), spec(PAIR_CH, PAIR_ST)
    sds = lambda r, c: jax.ShapeDtypeStruct((n_pairs, r, c), F32)
    return pl.pallas_call(
        functools.partial(_ssm_prep_kernel, chunk=chunk),
        grid=(n_pairs,),
        in_specs=[sub, sub, sub, lan, lan, lan, wide, wide, rows, rows, rows, rows, wide, wide],
        out_specs=[spec(width, width), spec(width, PAIR_ST), spec(width, PAIR_ST),
                   wide, wide, spec(2, PAIR_ST), spec(2, PAIR_ST)],
        out_shape=[sds(width, width), sds(width, PAIR_ST), sds(width, PAIR_ST),
                   sds(PAIR_ST, width), sds(PAIR_ST, width), sds(2, PAIR_ST), sds(2, PAIR_ST)],
        compiler_params=_params("arbitrary"),
        name="ssm_prep",
    )(*args)


def _ssm_loc_kernel(u_ref, e_re_ref, e_im_ref, l_re_ref, l_im_ref):
    ub = u_ref[...].astype(BF16)
    l_re_ref[...] = _dot(ub, e_re_ref[...].astype(BF16))
    l_im_ref[...] = _dot(ub, e_im_ref[...].astype(BF16))


def _ssm_loc(u_pairs, e_re, e_im):
    n_pairs, rows, width = u_pairs.shape
    st = pl.BlockSpec((rows, PAIR_ST), lambda i: (0, i))
    sds = jax.ShapeDtypeStruct((rows, n_pairs * PAIR_ST), F32)
    espec = pl.BlockSpec((None, width, PAIR_ST), lambda i: (i, 0, 0))
    return pl.pallas_call(
        _ssm_loc_kernel, grid=(n_pairs,),
        in_specs=[pl.BlockSpec((None, rows, width), lambda i: (i, 0, 0)), espec, espec],
        out_specs=[st, st], out_shape=[sds, sds],
        compiler_params=_params("arbitrary"), name="ssm_loc",
    )(u_pairs, e_re, e_im)


def _ssm_scan_kernel(l_re_ref, l_im_ref, p_re_ref, p_im_ref, s_re_ref, s_im_ref):
    p_re, p_im = p_re_ref[...], p_im_ref[...]
    n_chunks = l_re_ref.shape[0]

    def body(c, s):
        s_re, s_im = s
        s_re_ref[pl.ds(c, 1), :] = s_re
        s_im_ref[pl.ds(c, 1), :] = s_im
        n_re, n_im = _cmul(p_re, p_im, s_re, s_im)
        return n_re + l_re_ref[pl.ds(c, 1), :], n_im + l_im_ref[pl.ds(c, 1), :]

    zero = jnp.zeros_like(p_re)
    lax.fori_loop(0, n_chunks, body, (zero, zero))


def _ssm_scan(loc_re, loc_im, p_re, p_im):
    rows, n_state = loc_re.shape
    tc = 1024
    blk = pl.BlockSpec((rows, tc), lambda j: (0, j))
    par = pl.BlockSpec((1, tc), lambda j: (0, j))
    sds = jax.ShapeDtypeStruct((rows, n_state), F32)
    return pl.pallas_call(
        _ssm_scan_kernel, grid=(n_state // tc,),
        in_specs=[blk, blk, par, par], out_specs=[blk, blk], out_shape=[sds, sds],
        compiler_params=_params("arbitrary"), name="ssm_scan",
    )(loc_re, loc_im, p_re[:, 0].reshape(1, n_state), p_im[:, 0].reshape(1, n_state))


def _gelu_tanh(x):
    c = math.sqrt(2.0 / math.pi)
    return 0.5 * x * (1.0 + jnp.tanh(c * (x + 0.044715 * (x * x * x))))


def _ssm_out_kernel(u_ref, s_re_ref, s_im_ref, m_ref, e_re_ref, e_im_ref, f_re_ref, f_im_ref,
                    p_re_ref, p_im_ref, d_ref, y_ref, n_re_ref, n_im_ref):
    u = u_ref[...]
    ub = u.astype(BF16)
    s_re, s_im = s_re_ref[...], s_im_ref[...]
    y = (_dot_nt(ub, m_ref[...].astype(BF16))
         + _dot(s_re.astype(BF16), f_re_ref[...].astype(BF16))
         + _dot(s_im.astype(BF16), f_im_ref[...].astype(BF16))
         + d_ref[...] * u)
    y_ref[...] = _gelu_tanh(y)
    a_re, a_im = _cmul(p_re_ref[...], p_im_ref[...], s_re, s_im)
    n_re_ref[...] = a_re + _dot(ub, e_re_ref[...].astype(BF16))
    n_im_ref[...] = a_im + _dot(ub, e_im_ref[...].astype(BF16))


def _ssm_out(u_pairs, s_re, s_im, prep, d_tiled, half):
    n_pairs, rows, width = u_pairs.shape
    m, e_re, e_im, f_re, f_im, p_re, p_im = prep
    st = pl.BlockSpec((rows, PAIR_ST), lambda i: (0, i))
    per = lambda r, c: pl.BlockSpec((None, r, c), lambda i: (i, 0, 0))
    e_spec = pl.BlockSpec((None, width, PAIR_ST), lambda i: (i, 1 if half else 0, 0))
    p_spec = pl.BlockSpec((None, 1, PAIR_ST), lambda i: (i, 0, 0))
    sds = jax.ShapeDtypeStruct((rows, n_pairs * PAIR_ST), F32)
    pick = slice(1, 2) if half else slice(0, 1)
    return pl.pallas_call(
        _ssm_out_kernel, grid=(n_pairs,),
        in_specs=[per(rows, width), st, st, per(width, width), e_spec, e_spec,
                  per(PAIR_ST, width), per(PAIR_ST, width), p_spec, p_spec, per(1, width)],
        out_specs=[per(rows, width), st, st],
        out_shape=[jax.ShapeDtypeStruct((n_pairs, rows, width), F32), sds, sds],
        compiler_params=_params("arbitrary"), name="ssm_out",
    )(u_pairs, s_re, s_im, m, e_re, e_im, f_re, f_im, p_re[:, pick], p_im[:, pick], d_tiled)


def _to_pairs(u, rows, chunk):
    n_pairs = u.shape[1] // PAIR_CH
    return (u.reshape(rows, chunk, n_pairs, PAIR_CH).transpose(2, 0, 1, 3)
            .reshape(n_pairs, rows, chunk * PAIR_CH))


def _from_pairs(y, rows, chunk):
    n_pairs = y.shape[0]
    return (y.reshape(n_pairs, rows, chunk, PAIR_CH).transpose(1, 2, 0, 3)
            .reshape(rows * chunk, n_pairs * PAIR_CH))


def _glu_kernel(y_ref, w_ref, b_ref, o_ref):
    y = y_ref[...]
    t = _dot(y.astype(BF16), w_ref[...]) + b_ref[...]
    o_ref[...] = (y * jax.nn.sigmoid(t)).astype(BF16)


def _glu(y, w_b, layer, b):
    n, c = y.shape
    return pl.pallas_call(
        _glu_kernel, grid=(n // ROW_TILE,),
        in_specs=[pl.BlockSpec((ROW_TILE, c), lambda i: (i, 0)),
                  pl.BlockSpec((None, c, c), lambda i: (layer, 0, 0)),
                  pl.BlockSpec((1, c), lambda i: (0, 0))],
        out_specs=pl.BlockSpec((ROW_TILE, c), lambda i: (i, 0)),
        out_shape=jax.ShapeDtypeStruct((n, c), BF16),
        compiler_params=_params("arbitrary"), name="glu",
    )(y, w_b, b.reshape(1, c))


def _suffix_matrix(tk):
    j = jnp.arange(tk)[:, None]
    s = jnp.arange(tk + HEAD_DIM)[None, :]
    return ((j > s) | (s >= tk)).astype(BF16)


def _stick_weights(z2, carry, tri, mask, split):
    tk = z2.shape[1]
    neg_abs = lax.bitcast_convert_type(
        lax.bitcast_convert_type(z2, jnp.uint32) | jnp.uint32(0x80000000), F32)
    sp = jnp.maximum(z2, 0.0) + jnp.log2(1.0 + jnp.exp2(neg_abs))
    spm = sp if mask is None else jnp.where(mask, sp, 0.0)
    hi = spm.astype(BF16)
    tot = _dot(hi, tri)
    if split:
        tot = tot + _dot((spm - hi.astype(F32)).astype(BF16), tri)
    w = jnp.exp2((z2 - sp) - tot[:, :tk] - jnp.concatenate([carry] * (tk // HEAD_DIM), axis=1))
    if mask is not None:
        w = jnp.where(mask, w, 0.0)
    return w, carry + tot[:, tk:]


def _attn_prompt_kernel(bias_ref, q_ref, k_ref, v_ref, tri_ref, o_ref, carry_ref, acc_ref):
    hh = pl.program_id(0)
    i = pl.program_id(1)
    tq, tk = ATTN_TQ, ATTN_TK
    bias = bias_ref[hh]
    q = q_ref[...]
    tri = tri_ref[...]

    def tile(j, carry, acc, mask):
        k = k_ref[pl.ds(pl.multiple_of(j * tk, tk), tk), :]
        v = v_ref[pl.ds(pl.multiple_of(j * tk, tk), tk), :]
        w, carry = _stick_weights(_dot_nt(q, k) + bias, carry, tri, mask, split=False)
        return carry, acc + _dot(w.astype(BF16), v)

    def run(j0, count):
        st = carry_ref[...], acc_ref[...]
        for t in range(count):
            st = tile(j0 - t, st[0], st[1], None)
        carry_ref[...], acc_ref[...] = st

    row = lax.broadcasted_iota(jnp.int32, (tq, tk), 0)
    col = lax.broadcasted_iota(jnp.int32, (tq, tk), 1)
    zero = jnp.zeros((tq, HEAD_DIM), F32)
    carry_ref[...], acc_ref[...] = tile(i, zero, zero, col < row)

    group = PROMPT_TILES_PER_BODY
    n_groups = lax.div(i, group)

    def body(n, c):
        run(i - 1 - group * n, group)
        return c

    lax.fori_loop(0, n_groups, body, 0)
    left = i - group * n_groups
    part = group // 2
    while part >= 1:
        @pl.when((left & part) != 0)
        def _(part=part, left=left):
            run((left & (2 * part - 1)) - 1, part)
        part //= 2
    o_ref[...] = acc_ref[...].astype(BF16)


def _attn_prompt(q_b, k_b, v_b, bias2, seq):
    n_heads = q_b.shape[1] // HEAD_DIM
    kv = pl.BlockSpec((seq, HEAD_DIM), lambda h, i: (0, h))
    return pl.pallas_call(
        _attn_prompt_kernel,
        grid=(n_heads, seq // ATTN_TQ),
        in_specs=[pl.BlockSpec(memory_space=pltpu.SMEM),
                  pl.BlockSpec((ATTN_TQ, HEAD_DIM), lambda h, i: (i, h)),
                  kv, kv,
                  pl.BlockSpec((ATTN_TK, ATTN_TK + HEAD_DIM), lambda h, i: (0, 0))],
        out_specs=pl.BlockSpec((ATTN_TQ, HEAD_DIM), lambda h, i: (i, h)),
        out_shape=jax.ShapeDtypeStruct((seq, n_heads * HEAD_DIM), BF16),
        scratch_shapes=[pltpu.VMEM((ATTN_TQ, HEAD_DIM), F32), pltpu.VMEM((ATTN_TQ, HEAD_DIM), F32)],
        compiler_params=_params("arbitrary", "arbitrary"), name="attn_prompt",
    )(bias2, q_b, k_b, v_b, _suffix_matrix(ATTN_TK))


def _attn_sample_kernel(pt_ref, q_ref, kn_ref, vn_ref, *rest, n_heads, n_q, pages_per_step):
    kc_refs = rest[:pages_per_step]
    vc_refs = rest[pages_per_step:2 * pages_per_step]
    bias_ref, tri_ref, o_ref, carry_ref, acc_ref = rest[2 * pages_per_step:]
    p = pl.program_id(1)
    page = kn_ref.shape[0]
    rows = n_heads * n_q
    tri = tri_ref[...]
    bias = bias_ref[...]

    def head_q(hh):
        return q_ref[:, hh * HEAD_DIM:(hh + 1) * HEAD_DIM]

    def sweep(k_of, v_of, mask, carry, acc):
        z = jnp.concatenate([_dot_nt(head_q(hh), k_of(hh)) for hh in range(n_heads)], axis=0)
        w, carry = _stick_weights(z + bias, carry, tri, mask, split=True)
        return carry, acc + jnp.concatenate(
            [_dot(w[hh * n_q:(hh + 1) * n_q].astype(BF16), v_of(hh)) for hh in range(n_heads)], axis=0)

    @pl.when(p == 0)
    def _():
        q_idx = lax.broadcasted_iota(jnp.int32, (rows, page), 0) & (n_q - 1)
        key = lax.broadcasted_iota(jnp.int32, (rows, page), 1)
        zero = jnp.zeros((rows, HEAD_DIM), F32)
        carry_ref[...], acc_ref[...] = sweep(
            lambda hh: kn_ref[:, hh * HEAD_DIM:(hh + 1) * HEAD_DIM],
            lambda hh: vn_ref[:, hh * HEAD_DIM:(hh + 1) * HEAD_DIM], key < q_idx, zero, zero)

    def head_rows(ref, hh):
        return ref[pl.ds(hh, page, stride=n_heads), :].astype(BF16)

    pps = pages_per_step
    z = jnp.concatenate(
        [_dot_nt(head_q(hh), jnp.concatenate([head_rows(r, hh) for r in kc_refs], axis=0))
         for hh in range(n_heads)], axis=0)
    z2 = jnp.concatenate([z[:, r * page:(r + 1) * page] + bias for r in range(pps)], axis=0)
    neg_abs = lax.bitcast_convert_type(
        lax.bitcast_convert_type(z2, jnp.uint32) | jnp.uint32(0x80000000), F32)
    sp = jnp.maximum(z2, 0.0) + jnp.log2(1.0 + jnp.exp2(neg_abs))
    hi = sp.astype(BF16)
    tot = _dot(hi, tri) + _dot((sp - hi.astype(F32)).astype(BF16), tri)
    carry = carry_ref[...]
    carries = []
    for r in range(pps):
        carries.append(carry)
        carry = carry + tot[r * rows:(r + 1) * rows, page:]
    carry_ref[...] = carry
    w = jnp.exp2((z2 - sp) - tot[:, :page] - jnp.concatenate(carries, axis=0))
    acc_ref[...] += jnp.concatenate(
        [_dot(jnp.concatenate([w[r * rows + hh * n_q:r * rows + (hh + 1) * n_q] for r in range(pps)],
                              axis=1).astype(BF16),
              jnp.concatenate([head_rows(r, hh) for r in vc_refs], axis=0))
         for hh in range(n_heads)], axis=0)

    @pl.when(p == pl.num_programs(1) - 1)
    def _():
        acc = acc_ref[...]
        for hh in range(n_heads):
            o_ref[:, hh * HEAD_DIM:(hh + 1) * HEAD_DIM] = acc[hh * n_q:(hh + 1) * n_q]


SAMPLE_PAGES_PER_STEP = 8


def _attn_sample(q_s, k_new, v_new, cache_k, cache_v, layer, page_table, bias2):
    n_seq, n_q, width = q_s.shape
    n_heads = width // HEAD_DIM
    n_layers, n_pool, page = cache_k.shape[:3]
    n_pages = page_table.shape[1]
    pps = SAMPLE_PAGES_PER_STEP
    assert n_q & (n_q - 1) == 0 and n_pages % pps == 0
    rows = n_heads * n_q
    bias_rows = jnp.repeat(bias2, n_q).reshape(rows, 1)
    flat = (n_layers, n_pool, page * n_heads, HEAD_DIM)

    def cache(r):
        return pl.BlockSpec((None, None, page * n_heads, HEAD_DIM),
                            lambda b, p, pt: (layer, pt[b * n_pages + (n_pages - 1 - (p * pps + r))], 0, 0))

    new = pl.BlockSpec((None, page, width), lambda b, p, pt: (b, 0, 0))
    caches = [cache(r) for r in range(pps)]
    return pl.pallas_call(
        functools.partial(_attn_sample_kernel, n_heads=n_heads, n_q=n_q, pages_per_step=pps),
        grid_spec=pltpu.PrefetchScalarGridSpec(
            num_scalar_prefetch=1,
            grid=(n_seq, n_pages // pps),
            in_specs=[pl.BlockSpec((None, n_q, width), lambda b, p, pt: (b, 0, 0)),
                      new, new, *caches, *caches,
                      pl.BlockSpec((rows, 1), lambda b, p, pt: (0, 0)),
                      pl.BlockSpec((page, page + HEAD_DIM), lambda b, p, pt: (0, 0))],
            out_specs=pl.BlockSpec((None, n_q, width), lambda b, p, pt: (b, 0, 0)),
            scratch_shapes=[pltpu.VMEM((rows, HEAD_DIM), F32), pltpu.VMEM((rows, HEAD_DIM), F32)]),
        out_shape=jax.ShapeDtypeStruct((n_seq, n_q, width), F32),
        compiler_params=_params("arbitrary", "arbitrary"), name="attn_sample",
    )(page_table.reshape(-1), q_s, k_new, v_new,
      *([cache_k.reshape(flat)] * pps), *([cache_v.reshape(flat)] * pps), bias_rows, _suffix_matrix(page))


def _merge_kernel(a_ref, b_ref, ga_ref, gb_ref, wa_ref, wb_ref, o_ref):
    m = ga_ref[...] * _dot(a_ref[...], wa_ref[...]) + gb_ref[...] * _dot(b_ref[...], wb_ref[...])
    o_ref[...] = m.astype(BF16)


def _merge(a, b, gates, wa_b, wb_b, layer):
    n, c = a.shape
    d = wa_b.shape[2]
    row = pl.BlockSpec((ROW_TILE, c), lambda i: (i, 0))
    w = pl.BlockSpec((None, c, d), lambda i: (layer, 0, 0))
    return pl.pallas_call(
        _merge_kernel, grid=(n // ROW_TILE,),
        in_specs=[row, row,
                  pl.BlockSpec((ROW_TILE, d), lambda i: (i, 0)),
                  pl.BlockSpec((ROW_TILE, d), lambda i: (i, 1)), w, w],
        out_specs=pl.BlockSpec((ROW_TILE, d), lambda i: (i, 0)),
        out_shape=jax.ShapeDtypeStruct((n, d), BF16),
        compiler_params=_params("arbitrary"), name="merge",
    )(a, b, gates, gates, wa_b, wb_b)


def _route(logits):
    lane_i = lax.broadcasted_iota(jnp.int32, logits.shape, 1)
    lane = lane_i.astype(F32)
    neg = jnp.float32(-jnp.inf)
    big = jnp.float32(ROUTER_LANES)
    is_group = (lane_i >= N_EXPERTS) & (lane_i < N_EXPERTS + MOE_GROUPS)
    g_log = jnp.where(is_group, logits, neg)
    g_max = jnp.max(g_log, axis=-1, keepdims=True)
    g_idx = jnp.min(jnp.where(g_log == g_max, lane, big), axis=-1, keepdims=True) - N_EXPERTS
    g_prob = 1.0 / jnp.sum(jnp.where(is_group, jnp.exp(g_log - g_max), 0.0), axis=-1, keepdims=True)
    lane_group = lax.shift_right_logical(lane_i, 2).astype(F32)
    in_group = (lane_i < N_EXPERTS) & (lane_group == g_idx)
    assert EXPERTS_PER_GROUP == 4
    e_log = jnp.where(in_group, logits, neg)
    v1 = jnp.max(e_log, axis=-1, keepdims=True)
    i1 = jnp.min(jnp.where(e_log == v1, lane, big), axis=-1, keepdims=True)
    e_log2 = jnp.where(lane == i1, neg, e_log)
    v2 = jnp.max(e_log2, axis=-1, keepdims=True)
    i2 = jnp.min(jnp.where(e_log2 == v2, lane, big), axis=-1, keepdims=True)
    e2 = jnp.exp(v2 - v1)
    den = 1.0 + e2
    w1 = g_prob / den
    w2 = g_prob * e2 / den
    return jnp.where(lane == i1, w1, jnp.where(lane == i2, w2, 0.0))


def _out_kernel(m_ref, x_ref, w_ref, g1_ref, gn_ref, sc_ref, sh_ref, wr_ref, br_ref,
                x1_ref, h2_ref, gates_ref):
    x = x_ref[...]
    y3, g13 = _rows_mod(_dot(m_ref[...], w_ref[...]), g1_ref[...])
    x1 = x + (y3 * g13).reshape(x.shape)
    x1_ref[...] = x1
    h2 = _modulated_norm(x1, gn_ref[...], sc_ref[...], sh_ref[...])
    h2_ref[...] = h2.astype(BF16)
    gates_ref[...] = _route(_dot_exact(h2, wr_ref[...]) + br_ref[...])


def _out_proj(merged, x, w_out_b, layer, mod3, n_prompt_tiles, norm_g, w_router, b_router):
    n, d = x.shape
    row = pl.BlockSpec((ROW_TILE, d), lambda i: (i, 0))
    small = pl.BlockSpec((ROW_TILE, ROUTER_LANES), lambda i: (i, 0))
    return pl.pallas_call(
        _out_kernel, grid=(n // ROW_TILE,),
        in_specs=[row, row,
                  pl.BlockSpec((None, d, d), lambda i: (layer, 0, 0)),
                  _mod_spec(n_prompt_tiles, d, 2),
                  pl.BlockSpec((1, d), lambda i: (0, 0)),
                  _mod_spec(n_prompt_tiles, d, 4),
                  _mod_spec(n_prompt_tiles, d, 3),
                  pl.BlockSpec((d, ROUTER_LANES), lambda i: (0, 0)),
                  pl.BlockSpec((1, ROUTER_LANES), lambda i: (0, 0))],
        out_specs=[row, row, small],
        out_shape=[jax.ShapeDtypeStruct((n, d), F32), jax.ShapeDtypeStruct((n, d), BF16),
                   jax.ShapeDtypeStruct((n, ROUTER_LANES), F32)],
        compiler_params=_params("arbitrary"), name="out_proj",
    )(merged, x, w_out_b, mod3, norm_g.reshape(1, d), mod3, mod3, w_router, b_router)


def _moe_kernel(h_ref, x_ref, gates_ref, g2_ref, wg_ref, wu_ref, wd_ref, o_ref):
    e = pl.program_id(1)
    h = h_ref[...]
    gates = gates_ref[...]
    lane = lax.broadcasted_iota(jnp.int32, gates.shape, 1)
    gate = jnp.sum(jnp.where(lane == e, gates, 0.0), axis=-1, keepdims=True)
    hg = _dot(h, wg_ref[...])
    hu = _dot(h, wu_ref[...])
    act = (hg * jax.nn.sigmoid(hg)) * hu * gate
    contrib = _dot(act.astype(BF16), wd_ref[...])

    @pl.when(e == 0)
    def _():
        o_ref[...] = contrib

    @pl.when(e > 0)
    def _():
        o_ref[...] += contrib

    @pl.when(e == pl.num_programs(1) - 1)
    def _():
        f3, g23 = _rows_mod(o_ref[...], g2_ref[...])
        o_ref[...] = x_ref[...] + (f3 * g23).reshape(o_ref.shape)


def _moe(h2, x1, gates, mod3, n_prompt_tiles, wg_b, wu_b, wd_b, layer):
    n, d = x1.shape
    f = wg_b.shape[3]
    n_exp = wg_b.shape[1]
    row = pl.BlockSpec((ROW_TILE, d), lambda i, e: (i, 0))
    return pl.pallas_call(
        _moe_kernel, grid=(n // ROW_TILE, n_exp),
        in_specs=[row, row,
                  pl.BlockSpec((ROW_TILE, ROUTER_LANES), lambda i, e: (i, 0)),
                  pl.BlockSpec((None, ROW_TILE // SUBLANES, d),
                               lambda i, e: (jnp.where(i < n_prompt_tiles, 0, 1), 0, 5)),
                  pl.BlockSpec((None, None, d, f), lambda i, e: (layer, e, 0, 0)),
                  pl.BlockSpec((None, None, d, f), lambda i, e: (layer, e, 0, 0)),
                  pl.BlockSpec((None, None, f, d), lambda i, e: (layer, e, 0, 0))],
        out_specs=row,
        out_shape=jax.ShapeDtypeStruct((n, d), F32),
        compiler_params=_params("arbitrary", "arbitrary"), name="moe",
    )(h2, x1, gates, mod3, wg_b, wu_b, wd_b)


def kernel(x_prompt, x_sample, c_prompt, c_sample, cache_k, cache_v, state_ssm_re, state_ssm_im, page_table, w_ada, b_ada, norm_mix_g, w_in, ssm_a_re, ssm_a_im, ssm_log_dt, ssm_b_re, ssm_b_im, ssm_c_re, ssm_c_im, ssm_d, w_glu, b_glu, q_norm_g, k_norm_g, sb_bias, w_branch_ssm, w_branch_attn, w_out, norm_ffn_g, w_router_group, b_router_group, w_router_expert, b_router_expert, w_exp_gate, w_exp_up, w_exp_down):
    n_layers = w_ada.shape[0]
    bsz, seq, d = x_prompt.shape
    n_seq, n_q, _ = x_sample.shape
    assert bsz == 1 and n_q == SUBLANES and n_seq * n_q == ROW_TILE and seq % ROW_TILE == 0
    n_s = n_seq * n_q
    n_prompt_tiles = seq // ROW_TILE
    mod_rows = ROW_TILE // SUBLANES
    ssm_w = ssm_d.shape[1]
    attn_w = w_branch_attn.shape[1]
    assert ssm_w == attn_w
    n_heads = attn_w // HEAD_DIM
    n_groups = ssm_w // SSM_GROUP
    page = cache_k.shape[2]
    chunk_p, chunk_s = 16, n_q
    assert seq % chunk_p == 0 and chunk_p == 2 * chunk_s and page == HEAD_DIM

    w_in_b = w_in.astype(BF16)
    w_glu_b = w_glu.astype(BF16)
    w_bs_b = w_branch_ssm.astype(BF16)
    w_ba_b = w_branch_attn.astype(BF16)
    w_out_b = w_out.astype(BF16)
    wg_b, wu_b, wd_b = w_exp_gate.astype(BF16), w_exp_up.astype(BF16), w_exp_down.astype(BF16)

    c_all = jnp.concatenate([c_prompt, c_sample], axis=0)
    pad = -c_all.shape[0] % SUBLANES
    mod = _adaln(jnp.pad(c_all, ((0, pad), (0, 0))), w_ada, b_ada)

    x = jnp.concatenate([x_prompt.reshape(seq, d), x_sample.reshape(n_s, d)], axis=0)
    outs = {k: [] for k in ("kp", "vp", "rp", "ip", "ks", "vs", "rs", "is")}
    for l in range(n_layers):
        mod3 = jnp.stack([jnp.broadcast_to(mod[l, 0], (mod_rows, N_MOD * d)), mod[l, 1:1 + n_seq]])
        h = _norm_mod(x, norm_mix_g[l], mod3, n_prompt_tiles, 1, 0)
        u, q_b, k_t, k_b, v_t, v_b, gates = _in_proj(h, w_in_b, l, q_norm_g[l], k_norm_g[l], ssm_w)

        ssm_par = (ssm_a_re[l], ssm_a_im[l], ssm_log_dt[l], ssm_b_re[l], ssm_b_im[l], ssm_c_re[l], ssm_c_im[l])
        d_pairs = ssm_d[l].reshape(n_groups // 2, 1, PAIR_CH)
        prep_p = _ssm_prep(*ssm_par, chunk_p)
        up = _to_pairs(u[:seq], seq // chunk_p, chunk_p)
        loc_re, loc_im = _ssm_loc(up, prep_p[1], prep_p[2])
        sp_re, sp_im = _ssm_scan(loc_re, loc_im, prep_p[5], prep_p[6])
        yp, np_re, np_im = _ssm_out(up, sp_re, sp_im, prep_p, jnp.tile(d_pairs, (1, 1, chunk_p)), False)
        us = _to_pairs(u[seq:], n_seq, chunk_s)
        h0_re = state_ssm_re[l].reshape(n_seq, n_groups * SSM_STATE)
        h0_im = state_ssm_im[l].reshape(n_seq, n_groups * SSM_STATE)
        ys, ns_re, ns_im = _ssm_out(us, h0_re, h0_im, prep_p, jnp.tile(d_pairs, (1, 1, chunk_s)), True)
        y = jnp.concatenate([_from_pairs(yp, seq // chunk_p, chunk_p), _from_pairs(ys, n_seq, chunk_s)], axis=0)
        a_branch = _glu(y, w_glu_b, l, b_glu[l])

        bias2 = sb_bias[l].astype(F32) * LOG2_E
        bp = _attn_prompt(q_b, k_b, v_b, bias2, seq)
        new_pad = ((0, 0), (0, page - n_q), (0, 0))
        bs = _attn_sample(q_b[seq:].reshape(n_seq, n_q, attn_w),
                          jnp.pad(k_b[seq:].reshape(n_seq, n_q, attn_w), new_pad),
                          jnp.pad(v_b[seq:].reshape(n_seq, n_q, attn_w), new_pad),
                          cache_k, cache_v, l, page_table, bias2)
        b_branch = jnp.concatenate([bp, bs.reshape(n_s, attn_w).astype(BF16)], axis=0)

        merged = _merge(a_branch, b_branch, gates, w_bs_b, w_ba_b, l)
        w_router = jnp.pad(jnp.concatenate([w_router_expert[l], w_router_group[l]], axis=1),
                           ((0, 0), (0, ROUTER_LANES - N_EXPERTS - MOE_GROUPS)))
        b_router = jnp.pad(jnp.concatenate([b_router_expert[l], b_router_group[l]]),
                           (0, ROUTER_LANES - N_EXPERTS - MOE_GROUPS)).reshape(1, ROUTER_LANES)
        x1, h2, route = _out_proj(merged, x, w_out_b, l, mod3, n_prompt_tiles, norm_ffn_g[l], w_router, b_router)
        x = _moe(h2, x1, route, mod3, n_prompt_tiles, wg_b, wu_b, wd_b, l)

        outs["kp"].append(k_t[:seq * n_heads].reshape(1, seq, n_heads, HEAD_DIM))
        outs["vp"].append(v_t[:seq * n_heads].reshape(1, seq, n_heads, HEAD_DIM))
        outs["rp"].append(np_re[-1].reshape(1, n_groups, SSM_STATE))
        outs["ip"].append(np_im[-1].reshape(1, n_groups, SSM_STATE))
        outs["ks"].append(k_t[seq * n_heads:].reshape(n_seq, n_q, n_heads, HEAD_DIM))
        outs["vs"].append(v_t[seq * n_heads:].reshape(n_seq, n_q, n_heads, HEAD_DIM))
        outs["rs"].append(ns_re.reshape(n_seq, n_groups, SSM_STATE))
        outs["is"].append(ns_im.reshape(n_seq, n_groups, SSM_STATE))

    st = lambda k: jnp.stack(outs[k])
    return (x[:seq].reshape(1, seq, d), x[seq:].reshape(n_seq, n_q, d),
            st("kp"), st("vp"), st("rp"), st("ip"), st("ks"), st("vs"), st("rs"), st("is"))
```

```python
import functools
import math

import jax
import jax.numpy as jnp
from jax import lax
from jax.experimental import pallas as pl
from jax.experimental.pallas import tpu as pltpu

F32 = jnp.float32
BF16 = jnp.bfloat16

RMS_EPS = 1e-6
HEAD_DIM = 128
SSM_GROUP = 16
SSM_STATE = 64
PAIR_CH = 2 * SSM_GROUP
PAIR_ST = 2 * SSM_STATE
MOE_GROUPS = 4
EXPERTS_PER_GROUP = 4
N_EXPERTS = MOE_GROUPS * EXPERTS_PER_GROUP
ROUTER_LANES = 128
N_MOD = 6
ROW_TILE = 256
SUBLANES = 8
VMEM_LIMIT_BYTES = 56 * 1024 * 1024
ATTN_TQ = 256
ATTN_TK = 256
MOE_ROW_TILES = 8
PROMPT_TILES_PER_BODY = 8
LOG2_E = 1.4426950408889634
Q_SCALE_LOG2 = HEAD_DIM ** -0.5 * LOG2_E


def _params(*sem):
    return pltpu.CompilerParams(dimension_semantics=sem, vmem_limit_bytes=VMEM_LIMIT_BYTES)


def _dot(a, b):
    return jnp.dot(a, b, preferred_element_type=F32)


def _dot_nt(a, b):
    return lax.dot_general(a, b, (((1,), (1,)), ((), ())), preferred_element_type=F32)


def _dot_exact(a, b):
    return jnp.dot(a, b, preferred_element_type=F32, precision=lax.Precision.HIGHEST)


def _adaln_kernel(c_ref, w_ref, b_ref, o_ref):
    c = c_ref[...]
    a = (c * jax.nn.sigmoid(c)).astype(BF16)
    o_ref[...] = _dot(a, w_ref[...].astype(BF16)) + b_ref[...]


def _adaln(c_all, w_ada, b_ada):
    n_layers, d, n_out = w_ada.shape
    r = c_all.shape[0]
    tn = 1024
    return pl.pallas_call(
        _adaln_kernel,
        grid=(n_layers, n_out // tn),
        in_specs=[pl.BlockSpec((r, d), lambda l, j: (0, 0)),
                  pl.BlockSpec((None, d, tn), lambda l, j: (l, 0, j)),
                  pl.BlockSpec((None, 1, tn), lambda l, j: (l, 0, j))],
        out_specs=pl.BlockSpec((None, r, tn), lambda l, j: (l, 0, j)),
        out_shape=jax.ShapeDtypeStruct((n_layers, r, n_out), F32),
        compiler_params=_params("arbitrary", "arbitrary"),
        name="adaln",
    )(c_all, w_ada, b_ada.reshape(n_layers, 1, n_out))


def _rows_mod(v, m):
    tm, d = v.shape
    return v.reshape(tm // SUBLANES, SUBLANES, d), m[:, None, :]


def _modulated_norm(x, g, sc, sh):
    ms = jnp.mean(x * x, axis=-1, keepdims=True)
    y = x * lax.rsqrt(ms + RMS_EPS) * g
    y3, sc3 = _rows_mod(y, sc)
    h = y3 * (1.0 + sc3) + sh[:, None, :]
    return h.reshape(x.shape)


def _mod_spec(n_prompt_tiles, d, which):
    return pl.BlockSpec((None, ROW_TILE // SUBLANES, d),
                        lambda i: (jnp.where(i < n_prompt_tiles, 0, 1), 0, which))


def _norm_mod_kernel(x_ref, g_ref, sc_ref, sh_ref, o_ref):
    o_ref[...] = _modulated_norm(x_ref[...], g_ref[...], sc_ref[...], sh_ref[...]).astype(BF16)


def _res_norm_mod_kernel(x1_ref, f_ref, g2_ref, g_ref, sc_ref, sh_ref, x_ref, h_ref):
    f3, g23 = _rows_mod(f_ref[...], g2_ref[...])
    x = x1_ref[...] + (f3 * g23).reshape(x_ref.shape)
    x_ref[...] = x
    h_ref[...] = _modulated_norm(x, g_ref[...], sc_ref[...], sh_ref[...]).astype(BF16)


def _res_norm_mod(x1, ffn, mod3_prev, g, mod3, n_prompt_tiles):
    n, d = x1.shape
    row = pl.BlockSpec((ROW_TILE, d), lambda i: (i, 0))
    return pl.pallas_call(
        _res_norm_mod_kernel,
        grid=(n // ROW_TILE,),
        in_specs=[row, row, _mod_spec(n_prompt_tiles, d, 5),
                  pl.BlockSpec((1, d), lambda i: (0, 0)),
                  _mod_spec(n_prompt_tiles, d, 1), _mod_spec(n_prompt_tiles, d, 0)],
        out_specs=[row, row],
        out_shape=[jax.ShapeDtypeStruct((n, d), F32), jax.ShapeDtypeStruct((n, d), BF16)],
        compiler_params=_params("arbitrary"),
        name="res_norm_mod",
    )(x1, ffn, mod3_prev, g.reshape(1, d), mod3, mod3)


def _norm_mod(x, g, mod3, n_prompt_tiles, sc_idx, sh_idx):
    n, d = x.shape
    return pl.pallas_call(
        _norm_mod_kernel,
        grid=(n // ROW_TILE,),
        in_specs=[pl.BlockSpec((ROW_TILE, d), lambda i: (i, 0)),
                  pl.BlockSpec((1, d), lambda i: (0, 0)),
                  _mod_spec(n_prompt_tiles, d, sc_idx),
                  _mod_spec(n_prompt_tiles, d, sh_idx)],
        out_specs=pl.BlockSpec((ROW_TILE, d), lambda i: (i, 0)),
        out_shape=jax.ShapeDtypeStruct((n, d), BF16),
        compiler_params=_params("arbitrary"),
        name="norm_mod",
    )(x, g.reshape(1, d), mod3, mod3)


def _store_token_major(ref, val):
    tm = val.shape[0]
    n_heads = val.shape[1] // HEAD_DIM
    for hh in range(n_heads):
        ref[pl.ds(hh, tm, stride=n_heads), :] = val[:, hh * HEAD_DIM:(hh + 1) * HEAD_DIM]


def _proj_uv_kernel(h_ref, wu_ref, wv_ref, u_ref, vt_ref, vb_ref):
    h = h_ref[...]
    u_ref[...] = _dot(h, wu_ref[...])
    v = _dot(h, wv_ref[...])
    _store_token_major(vt_ref, v)
    vb_ref[...] = v.astype(BF16)


def _head_norm(acc, g):
    outs = []
    for hh in range(acc.shape[1] // HEAD_DIM):
        blk = acc[:, hh * HEAD_DIM:(hh + 1) * HEAD_DIM]
        ms = jnp.mean(blk * blk, axis=-1, keepdims=True)
        outs.append(blk * lax.rsqrt(ms + RMS_EPS) * g)
    return jnp.concatenate(outs, axis=1)


def _proj_qk_kernel(h_ref, wq_ref, wk_ref, gq_ref, gk_ref, qb_ref, kt_ref, kb_ref):
    h = h_ref[...]
    qb_ref[...] = (_head_norm(_dot(h, wq_ref[...]), gq_ref[...]) * Q_SCALE_LOG2).astype(BF16)
    k = _head_norm(_dot(h, wk_ref[...]), gk_ref[...])
    _store_token_major(kt_ref, k)
    kb_ref[...] = k.astype(BF16)


def _proj_gate_kernel(h_ref, w_ref, o_ref):
    o_ref[...] = jax.nn.sigmoid(_dot(h_ref[...], w_ref[...]))


def _in_proj(h, w_in_b, layer, q_g, k_g, width):
    n, d = h.shape
    nt = n // ROW_TILE
    row = pl.BlockSpec((ROW_TILE, d), lambda i: (i, 0))
    out = pl.BlockSpec((ROW_TILE, width), lambda i: (i, 0))

    def wcol(j):
        return pl.BlockSpec((None, d, width), lambda i: (layer, 0, j))

    sds = lambda dt: jax.ShapeDtypeStruct((n, width), dt)
    n_heads = width // HEAD_DIM
    tok = pl.BlockSpec((ROW_TILE * n_heads, HEAD_DIM), lambda i: (i, 0))
    tok_sds = jax.ShapeDtypeStruct((n * n_heads, HEAD_DIM), F32)
    u, v_t, v_b = pl.pallas_call(
        _proj_uv_kernel, grid=(nt,),
        in_specs=[row, wcol(0), wcol(3)], out_specs=[out, tok, out],
        out_shape=[sds(F32), tok_sds, sds(BF16)],
        compiler_params=_params("arbitrary"), name="proj_uv",
    )(h, w_in_b, w_in_b)
    gspec = pl.BlockSpec((1, HEAD_DIM), lambda i: (0, 0))
    q_b, k_t, k_b = pl.pallas_call(
        _proj_qk_kernel, grid=(nt,),
        in_specs=[row, wcol(1), wcol(2), gspec, gspec], out_specs=[out, tok, out],
        out_shape=[sds(BF16), tok_sds, sds(BF16)],
        compiler_params=_params("arbitrary"), name="proj_qk",
    )(h, w_in_b, w_in_b, q_g.reshape(1, HEAD_DIM), k_g.reshape(1, HEAD_DIM))
    n_gate = (w_in_b.shape[2] - 4 * width) // width
    gates = pl.pallas_call(
        _proj_gate_kernel, grid=(n_gate, nt),
        in_specs=[pl.BlockSpec((ROW_TILE, d), lambda j, i: (i, 0)),
                  pl.BlockSpec((None, d, width), lambda j, i: (layer, 0, 4 + j))],
        out_specs=pl.BlockSpec((ROW_TILE, width), lambda j, i: (i, j)),
        out_shape=jax.ShapeDtypeStruct((n, n_gate * width), F32),
        compiler_params=_params("arbitrary", "arbitrary"), name="proj_gate",
    )(h, w_in_b)
    return u, q_b, k_t, k_b, v_t, v_b, gates


def _cmul(ar, ai, br, bi):
    return ar * br - ai * bi, ar * bi + ai * br


def _discretise(a_re, a_im, log_dt):
    dt = jnp.exp(log_dt)
    mag = jnp.exp(dt * a_re)
    ab_re = mag * jnp.cos(dt * a_im)
    ab_im = mag * jnp.sin(dt * a_im)
    den = a_re * a_re + a_im * a_im
    f_re = ((ab_re - 1.0) * a_re + ab_im * a_im) / den
    f_im = (ab_im * a_re - (ab_re - 1.0) * a_im) / den
    return dt, f_re, f_im


def _abar_pow(a_re, a_im, dt, k):
    mag = jnp.exp(k * dt * a_re)
    th = k * dt * a_im
    return mag * jnp.cos(th), mag * jnp.sin(th)


def _ssm_prep_kernel(as_re_ref, as_im_ref, ldts_ref, al_re_ref, al_im_ref, ldtl_ref,
                     bt_re_ref, bt_im_ref, btr_re_ref, btr_im_ref, c_re_ref, c_im_ref,
                     ct_re_ref, ct_im_ref,
                     m_ref, e_re_ref, e_im_ref, f_re_ref, f_im_ref, p_re_ref, p_im_ref, *, chunk):
    width = chunk * PAIR_CH
    as_re, as_im = as_re_ref[...], as_im_ref[...]
    dts, fs_re, fs_im = _discretise(as_re, as_im, ldts_ref[...])
    al_re, al_im = al_re_ref[...], al_im_ref[...]
    dtl, fl_re, fl_im = _discretise(al_re, al_im, ldtl_ref[...])

    lane_blk = lax.shift_right_logical(lax.broadcasted_iota(jnp.int32, (1, width), 1), 5)
    assert PAIR_CH == 32

    k_rev = (chunk - 1 - lane_blk).astype(F32)
    pk_re, pk_im = _abar_pow(as_re, as_im, dts, k_rev)
    bb_re, bb_im = _cmul(fs_re, fs_im, bt_re_ref[...], bt_im_ref[...])
    w_re, w_im = _cmul(pk_re, pk_im, bb_re, bb_im)
    kr = _dot_exact(c_re_ref[...], w_re) - _dot_exact(c_im_ref[...], w_im)
    lane = lax.broadcasted_iota(jnp.int32, (PAIR_CH, width), 1)
    for t in range(chunk):
        shift = (chunk - 1 - t) * PAIR_CH
        rolled = kr if shift == 0 else pltpu.roll(kr, width - shift, axis=1)
        m_ref[t * PAIR_CH:(t + 1) * PAIR_CH, :] = jnp.where(lane < (t + 1) * PAIR_CH, rolled, 0.0)

    bbt_re, bbt_im = _cmul(fl_re, fl_im, btr_re_ref[...], btr_im_ref[...])
    for tau in range(chunk):
        q_re, q_im = _abar_pow(al_re, al_im, dtl, float(chunk - 1 - tau))
        e_re, e_im = _cmul(q_re, q_im, bbt_re, bbt_im)
        e_re_ref[tau * PAIR_CH:(tau + 1) * PAIR_CH, :] = e_re
        e_im_ref[tau * PAIR_CH:(tau + 1) * PAIR_CH, :] = e_im

    k_f = (lane_blk + 1).astype(F32)
    pf_re, pf_im = _abar_pow(as_re, as_im, dts, k_f)
    ca_re, ca_im = _cmul(ct_re_ref[...], ct_im_ref[...], pf_re, pf_im)
    f_re_ref[...] = ca_re
    f_im_ref[...] = -ca_im

    pt_re, pt_im = _abar_pow(al_re, al_im, dtl, float(chunk))
    ph_re, ph_im = _abar_pow(al_re, al_im, dtl, float(chunk // 2))
    p_re_ref[...] = jnp.concatenate([pt_re, ph_re], axis=0)
    p_im_ref[...] = jnp.concatenate([pt_im, ph_im], axis=0)


def _ssm_layouts(a_re, a_im, log_dt, b_re, b_im, c_re, c_im, chunk):
    n_groups = a_re.shape[0]
    n_pairs = n_groups // 2
    eye = jnp.eye(2, dtype=F32)

    def sub(x):
        return x.reshape(n_pairs, PAIR_ST, 1)

    def lan(x):
        return x.reshape(n_pairs, 1, PAIR_ST)

    ldt = jnp.broadcast_to(log_dt[:, None], (n_groups, SSM_STATE))

    def b_tiled(b):
        bp = b.reshape(n_pairs, 2, SSM_STATE, SSM_GROUP)
        bd = jnp.einsum('ngph,gk->ngpkh', bp, eye).reshape(n_pairs, PAIR_ST, PAIR_CH)
        return jnp.tile(bd, (1, 1, chunk))

    def b_rows(b):
        bp = b.reshape(n_pairs, 2, SSM_STATE, SSM_GROUP)
        return jnp.einsum('ngph,gk->nghkp', bp, eye).reshape(n_pairs, PAIR_CH, PAIR_ST)

    def c_rows(c):
        cp = c.reshape(n_pairs, 2, SSM_GROUP, SSM_STATE)
        return jnp.einsum('nghp,gk->nghkp', cp, eye).reshape(n_pairs, PAIR_CH, PAIR_ST)

    def c_tiled(c):
        cp = c.reshape(n_pairs, 2, SSM_GROUP, SSM_STATE)
        cd = jnp.einsum('nghp,gk->nkpgh', cp, eye).reshape(n_pairs, PAIR_ST, PAIR_CH)
        return jnp.tile(cd, (1, 1, chunk))

    return (sub(a_re), sub(a_im), sub(ldt), lan(a_re), lan(a_im), lan(ldt),
            b_tiled(b_re), b_tiled(b_im), b_rows(b_re), b_rows(b_im),
            c_rows(c_re), c_rows(c_im), c_tiled(c_re), c_tiled(c_im))


def _ssm_prep(a_re, a_im, log_dt, b_re, b_im, c_re, c_im, chunk):
    n_pairs = a_re.shape[0] // 2
    width = chunk * PAIR_CH
    args = _ssm_layouts(a_re, a_im, log_dt, b_re, b_im, c_re, c_im, chunk)

    def spec(r, c):
        return pl.BlockSpec((None, r, c), lambda i: (i, 0, 0))

    sub, lan = spec(PAIR_ST, 1), spec(1, PAIR_ST)
    wide, rows = spec(PAIR_ST, width), spec(PAIR_CH, PAIR_ST)
    sds = lambda r, c: jax.ShapeDtypeStruct((n_pairs, r, c), F32)
    return pl.pallas_call(
        functools.partial(_ssm_prep_kernel, chunk=chunk),
        grid=(n_pairs,),
        in_specs=[sub, sub, sub, lan, lan, lan, wide, wide, rows, rows, rows, rows, wide, wide],
        out_specs=[spec(width, width), spec(width, PAIR_ST), spec(width, PAIR_ST),
                   wide, wide, spec(2, PAIR_ST), spec(2, PAIR_ST)],
        out_shape=[sds(width, width), sds(width, PAIR_ST), sds(width, PAIR_ST),
                   sds(PAIR_ST, width), sds(PAIR_ST, width), sds(2, PAIR_ST), sds(2, PAIR_ST)],
        compiler_params=_params("arbitrary"),
        name="ssm_prep",
    )(*args)


def _ssm_loc_kernel(u_ref, e_re_ref, e_im_ref, l_re_ref, l_im_ref):
    ub = u_ref[...].astype(BF16)
    l_re_ref[...] = _dot(ub, e_re_ref[...].astype(BF16))
    l_im_ref[...] = _dot(ub, e_im_ref[...].astype(BF16))


def _ssm_loc(u_pairs, e_re, e_im):
    n_pairs, rows, width = u_pairs.shape
    st = pl.BlockSpec((rows, PAIR_ST), lambda i: (0, i))
    sds = jax.ShapeDtypeStruct((rows, n_pairs * PAIR_ST), F32)
    espec = pl.BlockSpec((None, width, PAIR_ST), lambda i: (i, 0, 0))
    return pl.pallas_call(
        _ssm_loc_kernel, grid=(n_pairs,),
        in_specs=[pl.BlockSpec((None, rows, width), lambda i: (i, 0, 0)), espec, espec],
        out_specs=[st, st], out_shape=[sds, sds],
        compiler_params=_params("arbitrary"), name="ssm_loc",
    )(u_pairs, e_re, e_im)


def _ssm_scan_kernel(l_re_ref, l_im_ref, p_re_ref, p_im_ref, s_re_ref, s_im_ref):
    p_re, p_im = p_re_ref[...], p_im_ref[...]
    n_chunks = l_re_ref.shape[0]

    def body(c, s):
        s_re, s_im = s
        s_re_ref[pl.ds(c, 1), :] = s_re
        s_im_ref[pl.ds(c, 1), :] = s_im
        n_re, n_im = _cmul(p_re, p_im, s_re, s_im)
        return n_re + l_re_ref[pl.ds(c, 1), :], n_im + l_im_ref[pl.ds(c, 1), :]

    zero = jnp.zeros_like(p_re)
    lax.fori_loop(0, n_chunks, body, (zero, zero))


def _ssm_scan(loc_re, loc_im, p_re, p_im):
    rows, n_state = loc_re.shape
    tc = 1024
    blk = pl.BlockSpec((rows, tc), lambda j: (0, j))
    par = pl.BlockSpec((1, tc), lambda j: (0, j))
    sds = jax.ShapeDtypeStruct((rows, n_state), F32)
    return pl.pallas_call(
        _ssm_scan_kernel, grid=(n_state // tc,),
        in_specs=[blk, blk, par, par], out_specs=[blk, blk], out_shape=[sds, sds],
        compiler_params=_params("arbitrary"), name="ssm_scan",
    )(loc_re, loc_im, p_re[:, 0].reshape(1, n_state), p_im[:, 0].reshape(1, n_state))


def _gelu_tanh(x):
    c = math.sqrt(2.0 / math.pi)
    return 0.5 * x * (1.0 + jnp.tanh(c * (x + 0.044715 * (x * x * x))))


def _ssm_out_kernel(u_ref, s_re_ref, s_im_ref, m_ref, e_re_ref, e_im_ref, f_re_ref, f_im_ref,
                    p_re_ref, p_im_ref, d_ref, y_ref, n_re_ref, n_im_ref):
    u = u_ref[...]
    ub = u.astype(BF16)
    s_re, s_im = s_re_ref[...], s_im_ref[...]
    y = (_dot_nt(ub, m_ref[...].astype(BF16))
         + _dot(s_re.astype(BF16), f_re_ref[...].astype(BF16))
         + _dot(s_im.astype(BF16), f_im_ref[...].astype(BF16))
         + d_ref[...] * u)
    y_ref[...] = _gelu_tanh(y)
    a_re, a_im = _cmul(p_re_ref[...], p_im_ref[...], s_re, s_im)
    n_re_ref[...] = a_re + _dot(ub, e_re_ref[...].astype(BF16))
    n_im_ref[...] = a_im + _dot(ub, e_im_ref[...].astype(BF16))


def _ssm_out(u_pairs, s_re, s_im, prep, d_tiled, half):
    n_pairs, rows, width = u_pairs.shape
    m, e_re, e_im, f_re, f_im, p_re, p_im = prep
    st = pl.BlockSpec((rows, PAIR_ST), lambda i: (0, i))
    per = lambda r, c: pl.BlockSpec((None, r, c), lambda i: (i, 0, 0))
    e_spec = pl.BlockSpec((None, width, PAIR_ST), lambda i: (i, 1 if half else 0, 0))
    p_spec = pl.BlockSpec((None, 1, PAIR_ST), lambda i: (i, 0, 0))
    sds = jax.ShapeDtypeStruct((rows, n_pairs * PAIR_ST), F32)
    pick = slice(1, 2) if half else slice(0, 1)
    return pl.pallas_call(
        _ssm_out_kernel, grid=(n_pairs,),
        in_specs=[per(rows, width), st, st, per(width, width), e_spec, e_spec,
                  per(PAIR_ST, width), per(PAIR_ST, width), p_spec, p_spec, per(1, width)],
        out_specs=[per(rows, width), st, st],
        out_shape=[jax.ShapeDtypeStruct((n_pairs, rows, width), F32), sds, sds],
        compiler_params=_params("arbitrary"), name="ssm_out",
    )(u_pairs, s_re, s_im, m, e_re, e_im, f_re, f_im, p_re[:, pick], p_im[:, pick], d_tiled)


def _to_pairs(u, rows, chunk):
    n_pairs = u.shape[1] // PAIR_CH
    return (u.reshape(rows, chunk, n_pairs, PAIR_CH).transpose(2, 0, 1, 3)
            .reshape(n_pairs, rows, chunk * PAIR_CH))


def _from_pairs(y, rows, chunk):
    n_pairs = y.shape[0]
    return (y.reshape(n_pairs, rows, chunk, PAIR_CH).transpose(1, 2, 0, 3)
            .reshape(rows * chunk, n_pairs * PAIR_CH))


def _glu_kernel(y_ref, w_ref, b_ref, o_ref):
    y = y_ref[...]
    t = _dot(y.astype(BF16), w_ref[...]) + b_ref[...]
    o_ref[...] = (y * jax.nn.sigmoid(t)).astype(BF16)


def _glu(y, w_b, layer, b):
    n, c = y.shape
    return pl.pallas_call(
        _glu_kernel, grid=(n // ROW_TILE,),
        in_specs=[pl.BlockSpec((ROW_TILE, c), lambda i: (i, 0)),
                  pl.BlockSpec((None, c, c), lambda i: (layer, 0, 0)),
                  pl.BlockSpec((1, c), lambda i: (0, 0))],
        out_specs=pl.BlockSpec((ROW_TILE, c), lambda i: (i, 0)),
        out_shape=jax.ShapeDtypeStruct((n, c), BF16),
        compiler_params=_params("arbitrary"), name="glu",
    )(y, w_b, b.reshape(1, c))


def _suffix_matrix(tk):
    j = jnp.arange(tk)[:, None]
    s = jnp.arange(tk + HEAD_DIM)[None, :]
    return ((j > s) | (s >= tk)).astype(BF16)


def _stick_weights(z2, carry, tri, mask, split):
    tk = z2.shape[1]
    neg_abs = lax.bitcast_convert_type(
        lax.bitcast_convert_type(z2, jnp.uint32) | jnp.uint32(0x80000000), F32)
    sp = jnp.maximum(z2, 0.0) + jnp.log2(1.0 + jnp.exp2(neg_abs))
    spm = sp if mask is None else jnp.where(mask, sp, 0.0)
    hi = spm.astype(BF16)
    tot = _dot(hi, tri)
    if split:
        tot = tot + _dot((spm - hi.astype(F32)).astype(BF16), tri)
    w = jnp.exp2((z2 - sp) - tot[:, :tk] - jnp.concatenate([carry] * (tk // HEAD_DIM), axis=1))
    if mask is not None:
        w = jnp.where(mask, w, 0.0)
    return w, carry + tot[:, tk:]


def _attn_prompt_kernel(bias_ref, q_ref, k_ref, v_ref, tri_ref, o_ref, carry_ref, acc_ref):
    hh = pl.program_id(0)
    i = pl.program_id(1)
    tq, tk = ATTN_TQ, ATTN_TK
    bias = bias_ref[hh]
    q = q_ref[...]
    tri = tri_ref[...]

    def tile(j, carry, acc, mask):
        k = k_ref[pl.ds(pl.multiple_of(j * tk, tk), tk), :]
        v = v_ref[pl.ds(pl.multiple_of(j * tk, tk), tk), :]
        w, carry = _stick_weights(_dot_nt(q, k) + bias, carry, tri, mask, split=False)
        return carry, acc + _dot(w.astype(BF16), v)

    def run(j0, count):
        st = carry_ref[...], acc_ref[...]
        for t in range(count):
            st = tile(j0 - t, st[0], st[1], None)
        carry_ref[...], acc_ref[...] = st

    row = lax.broadcasted_iota(jnp.int32, (tq, tk), 0)
    col = lax.broadcasted_iota(jnp.int32, (tq, tk), 1)
    zero = jnp.zeros((tq, HEAD_DIM), F32)
    carry_ref[...], acc_ref[...] = tile(i, zero, zero, col < row)

    group = PROMPT_TILES_PER_BODY
    n_groups = lax.div(i, group)

    def body(n, c):
        run(i - 1 - group * n, group)
        return c

    lax.fori_loop(0, n_groups, body, 0)
    left = i - group * n_groups
    part = group // 2
    while part >= 1:
        @pl.when((left & part) != 0)
        def _(part=part, left=left):
            run((left & (2 * part - 1)) - 1, part)
        part //= 2
    o_ref[...] = acc_ref[...].astype(BF16)


def _attn_prompt(q_b, k_b, v_b, bias2, seq):
    n_heads = q_b.shape[1] // HEAD_DIM
    kv = pl.BlockSpec((seq, HEAD_DIM), lambda h, i: (0, h))
    return pl.pallas_call(
        _attn_prompt_kernel,
        grid=(n_heads, seq // ATTN_TQ),
        in_specs=[pl.BlockSpec(memory_space=pltpu.SMEM),
                  pl.BlockSpec((ATTN_TQ, HEAD_DIM), lambda h, i: (i, h)),
                  kv, kv,
                  pl.BlockSpec((ATTN_TK, ATTN_TK + HEAD_DIM), lambda h, i: (0, 0))],
        out_specs=pl.BlockSpec((ATTN_TQ, HEAD_DIM), lambda h, i: (i, h)),
        out_shape=jax.ShapeDtypeStruct((seq, n_heads * HEAD_DIM), BF16),
        scratch_shapes=[pltpu.VMEM((ATTN_TQ, HEAD_DIM), F32), pltpu.VMEM((ATTN_TQ, HEAD_DIM), F32)],
        compiler_params=_params("arbitrary", "arbitrary"), name="attn_prompt",
    )(bias2, q_b, k_b, v_b, _suffix_matrix(ATTN_TK))


def _attn_sample_kernel(pt_ref, q_ref, kn_ref, vn_ref, *rest, n_heads, n_q, pages_per_step):
    kc_refs = rest[:pages_per_step]
    vc_refs = rest[pages_per_step:2 * pages_per_step]
    bias_ref, tri_ref, o_ref, carry_ref, acc_ref = rest[2 * pages_per_step:]
    p = pl.program_id(1)
    page = kn_ref.shape[0]
    rows = n_heads * n_q
    tri = tri_ref[...]
    bias = bias_ref[...]

    def head_q(hh):
        return q_ref[:, hh * HEAD_DIM:(hh + 1) * HEAD_DIM]

    def sweep(k_of, v_of, mask, carry, acc):
        z = jnp.concatenate([_dot_nt(head_q(hh), k_of(hh)) for hh in range(n_heads)], axis=0)
        w, carry = _stick_weights(z + bias, carry, tri, mask, split=True)
        return carry, acc + jnp.concatenate(
            [_dot(w[hh * n_q:(hh + 1) * n_q].astype(BF16), v_of(hh)) for hh in range(n_heads)], axis=0)

    @pl.when(p == 0)
    def _():
        q_idx = lax.broadcasted_iota(jnp.int32, (rows, page), 0) & (n_q - 1)
        key = lax.broadcasted_iota(jnp.int32, (rows, page), 1)
        zero = jnp.zeros((rows, HEAD_DIM), F32)
        carry_ref[...], acc_ref[...] = sweep(
            lambda hh: kn_ref[:, hh * HEAD_DIM:(hh + 1) * HEAD_DIM],
            lambda hh: vn_ref[:, hh * HEAD_DIM:(hh + 1) * HEAD_DIM], key < q_idx, zero, zero)

    def head_rows(ref, hh):
        return ref[pl.ds(hh, page, stride=n_heads), :].astype(BF16)

    pps = pages_per_step
    z = jnp.concatenate(
        [_dot_nt(head_q(hh), jnp.concatenate([head_rows(r, hh) for r in kc_refs], axis=0))
         for hh in range(n_heads)], axis=0)
    z2 = jnp.concatenate([z[:, r * page:(r + 1) * page] + bias for r in range(pps)], axis=0)
    neg_abs = lax.bitcast_convert_type(
        lax.bitcast_convert_type(z2, jnp.uint32) | jnp.uint32(0x80000000), F32)
    sp = jnp.maximum(z2, 0.0) + jnp.log2(1.0 + jnp.exp2(neg_abs))
    hi = sp.astype(BF16)
    tot = _dot(hi, tri) + _dot((sp - hi.astype(F32)).astype(BF16), tri)
    carry = carry_ref[...]
    carries = []
    for r in range(pps):
        carries.append(carry)
        carry = carry + tot[r * rows:(r + 1) * rows, page:]
    carry_ref[...] = carry
    w = jnp.exp2((z2 - sp) - tot[:, :page] - jnp.concatenate(carries, axis=0))
    acc_ref[...] += jnp.concatenate(
        [_dot(jnp.concatenate([w[r * rows + hh * n_q:r * rows + (hh + 1) * n_q] for r in range(pps)],
                              axis=1).astype(BF16),
              jnp.concatenate([head_rows(r, hh) for r in vc_refs], axis=0))
         for hh in range(n_heads)], axis=0)

    @pl.when(p == pl.num_programs(1) - 1)
    def _():
        acc = acc_ref[...]
        for hh in range(n_heads):
            o_ref[:, hh * HEAD_DIM:(hh + 1) * HEAD_DIM] = acc[hh * n_q:(hh + 1) * n_q]


SAMPLE_PAGES_PER_STEP = 8


def _attn_sample(q_s, k_new, v_new, cache_k, cache_v, layer, page_table, bias2):
    n_seq, n_q, width = q_s.shape
    n_heads = width // HEAD_DIM
    n_layers, n_pool, page = cache_k.shape[:3]
    n_pages = page_table.shape[1]
    pps = SAMPLE_PAGES_PER_STEP
    assert n_q & (n_q - 1) == 0 and n_pages % pps == 0
    rows = n_heads * n_q
    bias_rows = jnp.repeat(bias2, n_q).reshape(rows, 1)
    flat = (n_layers, n_pool, page * n_heads, HEAD_DIM)

    def cache(r):
        return pl.BlockSpec((None, None, page * n_heads, HEAD_DIM),
                            lambda b, p, pt: (layer, pt[b * n_pages + (n_pages - 1 - (p * pps + r))], 0, 0))

    new = pl.BlockSpec((None, page, width), lambda b, p, pt: (b, 0, 0))
    caches = [cache(r) for r in range(pps)]
    return pl.pallas_call(
        functools.partial(_attn_sample_kernel, n_heads=n_heads, n_q=n_q, pages_per_step=pps),
        grid_spec=pltpu.PrefetchScalarGridSpec(
            num_scalar_prefetch=1,
            grid=(n_seq, n_pages // pps),
            in_specs=[pl.BlockSpec((None, n_q, width), lambda b, p, pt: (b, 0, 0)),
                      new, new, *caches, *caches,
                      pl.BlockSpec((rows, 1), lambda b, p, pt: (0, 0)),
                      pl.BlockSpec((page, page + HEAD_DIM), lambda b, p, pt: (0, 0))],
            out_specs=pl.BlockSpec((None, n_q, width), lambda b, p, pt: (b, 0, 0)),
            scratch_shapes=[pltpu.VMEM((rows, HEAD_DIM), F32), pltpu.VMEM((rows, HEAD_DIM), F32)]),
        out_shape=jax.ShapeDtypeStruct((n_seq, n_q, width), F32),
        compiler_params=_params("arbitrary", "arbitrary"), name="attn_sample",
    )(page_table.reshape(-1), q_s, k_new, v_new,
      *([cache_k.reshape(flat)] * pps), *([cache_v.reshape(flat)] * pps), bias_rows, _suffix_matrix(page))


def _merge_kernel(a_ref, b_ref, ga_ref, gb_ref, wa_ref, wb_ref, o_ref):
    m = ga_ref[...] * _dot(a_ref[...], wa_ref[...]) + gb_ref[...] * _dot(b_ref[...], wb_ref[...])
    o_ref[...] = m.astype(BF16)


def _merge(a, b, gates, wa_b, wb_b, layer):
    n, c = a.shape
    d = wa_b.shape[2]
    row = pl.BlockSpec((ROW_TILE, c), lambda i: (i, 0))
    w = pl.BlockSpec((None, c, d), lambda i: (layer, 0, 0))
    return pl.pallas_call(
        _merge_kernel, grid=(n // ROW_TILE,),
        in_specs=[row, row,
                  pl.BlockSpec((ROW_TILE, d), lambda i: (i, 0)),
                  pl.BlockSpec((ROW_TILE, d), lambda i: (i, 1)), w, w],
        out_specs=pl.BlockSpec((ROW_TILE, d), lambda i: (i, 0)),
        out_shape=jax.ShapeDtypeStruct((n, d), BF16),
        compiler_params=_params("arbitrary"), name="merge",
    )(a, b, gates, gates, wa_b, wb_b)


def _route(logits):
    lane_i = lax.broadcasted_iota(jnp.int32, logits.shape, 1)
    lane = lane_i.astype(F32)
    neg = jnp.float32(-jnp.inf)
    big = jnp.float32(ROUTER_LANES)
    is_group = (lane_i >= N_EXPERTS) & (lane_i < N_EXPERTS + MOE_GROUPS)
    g_log = jnp.where(is_group, logits, neg)
    g_max = jnp.max(g_log, axis=-1, keepdims=True)
    g_idx = jnp.min(jnp.where(g_log == g_max, lane, big), axis=-1, keepdims=True) - N_EXPERTS
    g_prob = 1.0 / jnp.sum(jnp.where(is_group, jnp.exp(g_log - g_max), 0.0), axis=-1, keepdims=True)
    lane_group = lax.shift_right_logical(lane_i, 2).astype(F32)
    in_group = (lane_i < N_EXPERTS) & (lane_group == g_idx)
    assert EXPERTS_PER_GROUP == 4
    e_log = jnp.where(in_group, logits, neg)
    v1 = jnp.max(e_log, axis=-1, keepdims=True)
    i1 = jnp.min(jnp.where(e_log == v1, lane, big), axis=-1, keepdims=True)
    e_log2 = jnp.where(lane == i1, neg, e_log)
    v2 = jnp.max(e_log2, axis=-1, keepdims=True)
    i2 = jnp.min(jnp.where(e_log2 == v2, lane, big), axis=-1, keepdims=True)
    e2 = jnp.exp(v2 - v1)
    den = 1.0 + e2
    w1 = g_prob / den
    w2 = g_prob * e2 / den
    return jnp.where(lane == i1, w1, jnp.where(lane == i2, w2, 0.0))


def _out_kernel(m_ref, x_ref, w_ref, g1_ref, gn_ref, sc_ref, sh_ref, wr_hi_ref, wr_lo_ref, br_ref,
                x1_ref, h2_ref, gates_ref):
    x = x_ref[...]
    y3, g13 = _rows_mod(_dot(m_ref[...], w_ref[...]), g1_ref[...])
    x1 = x + (y3 * g13).reshape(x.shape)
    x1_ref[...] = x1
    h2 = _modulated_norm(x1, gn_ref[...], sc_ref[...], sh_ref[...])
    hi = h2.astype(BF16)
    h2_ref[...] = hi
    lo = (h2 - hi.astype(F32)).astype(BF16)
    wr_hi = wr_hi_ref[...]
    logits = _dot(hi, wr_hi) + (_dot(lo, wr_hi) + _dot(hi, wr_lo_ref[...]))
    gates_ref[...] = _route(logits + br_ref[...])


def _out_proj(merged, x, w_out_b, layer, mod3, n_prompt_tiles, norm_g, w_router, b_router):
    n, d = x.shape
    w_router_hi = w_router.astype(BF16)
    row = pl.BlockSpec((ROW_TILE, d), lambda i: (i, 0))
    small = pl.BlockSpec((ROW_TILE, ROUTER_LANES), lambda i: (i, 0))
    return pl.pallas_call(
        _out_kernel, grid=(n // ROW_TILE,),
        in_specs=[row, row,
                  pl.BlockSpec((None, d, d), lambda i: (layer, 0, 0)),
                  _mod_spec(n_prompt_tiles, d, 2),
                  pl.BlockSpec((1, d), lambda i: (0, 0)),
                  _mod_spec(n_prompt_tiles, d, 4),
                  _mod_spec(n_prompt_tiles, d, 3),
                  pl.BlockSpec((d, ROUTER_LANES), lambda i: (0, 0)),
                  pl.BlockSpec((d, ROUTER_LANES), lambda i: (0, 0)),
                  pl.BlockSpec((1, ROUTER_LANES), lambda i: (0, 0))],
        out_specs=[row, row, small],
        out_shape=[jax.ShapeDtypeStruct((n, d), F32), jax.ShapeDtypeStruct((n, d), BF16),
                   jax.ShapeDtypeStruct((n, ROUTER_LANES), F32)],
        compiler_params=_params("arbitrary"), name="out_proj",
    )(merged, x, w_out_b, mod3, norm_g.reshape(1, d), mod3, mod3, w_router_hi,
      (w_router - w_router_hi.astype(F32)).astype(BF16), b_router)


def _moe_kernel(h_ref, gates_ref, wg_ref, wu_ref, wd_ref, o_ref):
    e = pl.program_id(1)
    h = h_ref[...]
    gates = gates_ref[...]
    lane = lax.broadcasted_iota(jnp.int32, gates.shape, 1)
    gate = jnp.sum(jnp.where(lane == e, gates, 0.0), axis=-1, keepdims=True)
    hg = _dot(h, wg_ref[...])
    hu = _dot(h, wu_ref[...])
    act = (hg * jax.nn.sigmoid(hg)) * hu * gate
    contrib = _dot(act.astype(BF16), wd_ref[...])

    @pl.when(e == 0)
    def _():
        o_ref[...] = contrib

    @pl.when(e > 0)
    def _():
        o_ref[...] += contrib


def _moe(h2, gates, wg_b, wu_b, wd_b, layer):
    n, d = h2.shape
    f = wg_b.shape[3]
    n_exp = wg_b.shape[1]
    assert n % MOE_ROW_TILES == 0
    tm = n // MOE_ROW_TILES
    assert tm % 16 == 0
    row = pl.BlockSpec((tm, d), lambda i, e: (i, 0))
    return pl.pallas_call(
        _moe_kernel, grid=(MOE_ROW_TILES, n_exp),
        in_specs=[row,
                  pl.BlockSpec((tm, ROUTER_LANES), lambda i, e: (i, 0)),
                  pl.BlockSpec((None, None, d, f), lambda i, e: (layer, e, 0, 0)),
                  pl.BlockSpec((None, None, d, f), lambda i, e: (layer, e, 0, 0)),
                  pl.BlockSpec((None, None, f, d), lambda i, e: (layer, e, 0, 0))],
        out_specs=row,
        out_shape=jax.ShapeDtypeStruct((n, d), F32),
        compiler_params=_params("arbitrary", "arbitrary"), name="moe",
    )(h2, gates, wg_b, wu_b, wd_b)


def _residual_kernel(x_ref, f_ref, g2_ref, o_ref):
    f3, g23 = _rows_mod(f_ref[...], g2_ref[...])
    o_ref[...] = x_ref[...] + (f3 * g23).reshape(o_ref.shape)


def _residual(x1, ffn, mod3, n_prompt_tiles):
    n, d = x1.shape
    row = pl.BlockSpec((ROW_TILE, d), lambda i: (i, 0))
    return pl.pallas_call(
        _residual_kernel, grid=(n // ROW_TILE,),
        in_specs=[row, row, _mod_spec(n_prompt_tiles, d, 5)],
        out_specs=row, out_shape=jax.ShapeDtypeStruct((n, d), F32),
        compiler_params=_params("arbitrary"), name="residual",
    )(x1, ffn, mod3)


def kernel(x_prompt, x_sample, c_prompt, c_sample, cache_k, cache_v, state_ssm_re, state_ssm_im, page_table, w_ada, b_ada, norm_mix_g, w_in, ssm_a_re, ssm_a_im, ssm_log_dt, ssm_b_re, ssm_b_im, ssm_c_re, ssm_c_im, ssm_d, w_glu, b_glu, q_norm_g, k_norm_g, sb_bias, w_branch_ssm, w_branch_attn, w_out, norm_ffn_g, w_router_group, b_router_group, w_router_expert, b_router_expert, w_exp_gate, w_exp_up, w_exp_down):
    n_layers = w_ada.shape[0]
    bsz, seq, d = x_prompt.shape
    n_seq, n_q, _ = x_sample.shape
    assert bsz == 1 and n_q == SUBLANES and n_seq * n_q == ROW_TILE and seq % ROW_TILE == 0
    n_s = n_seq * n_q
    n_prompt_tiles = seq // ROW_TILE
    mod_rows = ROW_TILE // SUBLANES
    ssm_w = ssm_d.shape[1]
    attn_w = w_branch_attn.shape[1]
    assert ssm_w == attn_w
    n_heads = attn_w // HEAD_DIM
    n_groups = ssm_w // SSM_GROUP
    page = cache_k.shape[2]
    chunk_p, chunk_s = 16, n_q
    assert seq % chunk_p == 0 and chunk_p == 2 * chunk_s and page == HEAD_DIM

    w_in_b = w_in.astype(BF16)
    w_glu_b = w_glu.astype(BF16)
    w_bs_b = w_branch_ssm.astype(BF16)
    w_ba_b = w_branch_attn.astype(BF16)
    w_out_b = w_out.astype(BF16)
    wg_b, wu_b, wd_b = w_exp_gate.astype(BF16), w_exp_up.astype(BF16), w_exp_down.astype(BF16)

    c_all = jnp.concatenate([c_prompt, c_sample], axis=0)
    pad = -c_all.shape[0] % SUBLANES
    mod = _adaln(jnp.pad(c_all, ((0, pad), (0, 0))), w_ada, b_ada)

    x = jnp.concatenate([x_prompt.reshape(seq, d), x_sample.reshape(n_s, d)], axis=0)
    outs = {k: [] for k in ("kp", "vp", "rp", "ip", "ks", "vs", "rs", "is")}
    for l in range(n_layers):
        mod3_prev = mod3 if l else None
        mod3 = jnp.stack([jnp.broadcast_to(mod[l, 0], (mod_rows, N_MOD * d)), mod[l, 1:1 + n_seq]])
        if l == 0:
            h = _norm_mod(x, norm_mix_g[l], mod3, n_prompt_tiles, 1, 0)
        else:
            x, h = _res_norm_mod(x1, ffn, mod3_prev, norm_mix_g[l], mod3, n_prompt_tiles)
        u, q_b, k_t, k_b, v_t, v_b, gates = _in_proj(h, w_in_b, l, q_norm_g[l], k_norm_g[l], ssm_w)

        ssm_par = (ssm_a_re[l], ssm_a_im[l], ssm_log_dt[l], ssm_b_re[l], ssm_b_im[l], ssm_c_re[l], ssm_c_im[l])
        d_pairs = ssm_d[l].reshape(n_groups // 2, 1, PAIR_CH)
        prep_p = _ssm_prep(*ssm_par, chunk_p)
        up = _to_pairs(u[:seq], seq // chunk_p, chunk_p)
        loc_re, loc_im = _ssm_loc(up, prep_p[1], prep_p[2])
        sp_re, sp_im = _ssm_scan(loc_re, loc_im, prep_p[5], prep_p[6])
        yp, np_re, np_im = _ssm_out(up, sp_re, sp_im, prep_p, jnp.tile(d_pairs, (1, 1, chunk_p)), False)
        us = _to_pairs(u[seq:], n_seq, chunk_s)
        h0_re = state_ssm_re[l].reshape(n_seq, n_groups * SSM_STATE)
        h0_im = state_ssm_im[l].reshape(n_seq, n_groups * SSM_STATE)
        ys, ns_re, ns_im = _ssm_out(us, h0_re, h0_im, prep_p, jnp.tile(d_pairs, (1, 1, chunk_s)), True)
        y = jnp.concatenate([_from_pairs(yp, seq // chunk_p, chunk_p), _from_pairs(ys, n_seq, chunk_s)], axis=0)
        a_branch = _glu(y, w_glu_b, l, b_glu[l])

        bias2 = sb_bias[l].astype(F32) * LOG2_E
        bp = _attn_prompt(q_b, k_b, v_b, bias2, seq)
        new_pad = ((0, 0), (0, page - n_q), (0, 0))
        bs = _attn_sample(q_b[seq:].reshape(n_seq, n_q, attn_w),
                          jnp.pad(k_b[seq:].reshape(n_seq, n_q, attn_w), new_pad),
                          jnp.pad(v_b[seq:].reshape(n_seq, n_q, attn_w), new_pad),
                          cache_k, cache_v, l, page_table, bias2)
        b_branch = jnp.concatenate([bp, bs.reshape(n_s, attn_w).astype(BF16)], axis=0)

        merged = _merge(a_branch, b_branch, gates, w_bs_b, w_ba_b, l)
        w_router = jnp.pad(jnp.concatenate([w_router_expert[l], w_router_group[l]], axis=1),
                           ((0, 0), (0, ROUTER_LANES - N_EXPERTS - MOE_GROUPS)))
        b_router = jnp.pad(jnp.concatenate([b_router_expert[l], b_router_group[l]]),
                           (0, ROUTER_LANES - N_EXPERTS - MOE_GROUPS)).reshape(1, ROUTER_LANES)
        x1, h2, route = _out_proj(merged, x, w_out_b, l, mod3, n_prompt_tiles, norm_ffn_g[l], w_router, b_router)
        ffn = _moe(h2, route, wg_b, wu_b, wd_b, l)

        outs["kp"].append(k_t[:seq * n_heads].reshape(1, seq, n_heads, HEAD_DIM))
        outs["vp"].append(v_t[:seq * n_heads].reshape(1, seq, n_heads, HEAD_DIM))
        outs["rp"].append(np_re[-1].reshape(1, n_groups, SSM_STATE))
        outs["ip"].append(np_im[-1].reshape(1, n_groups, SSM_STATE))
        outs["ks"].append(k_t[seq * n_heads:].reshape(n_seq, n_q, n_heads, HEAD_DIM))
        outs["vs"].append(v_t[seq * n_heads:].reshape(n_seq, n_q, n_heads, HEAD_DIM))
        outs["rs"].append(ns_re.reshape(n_seq, n_groups, SSM_STATE))
        outs["is"].append(ns_im.reshape(n_seq, n_groups, SSM_STATE))

    x = _residual(x1, ffn, mod3, n_prompt_tiles)
    st = lambda k: jnp.stack(outs[k])
    return (x[:seq].reshape(1, seq, d), x[seq:].reshape(n_seq, n_q, d),
            st("kp"), st("vp"), st("rp"), st("ip"), st("ks"), st("vs"), st("rs"), st("is"))
```

```python
import functools
import math

import jax
import jax.numpy as jnp
from jax import lax
from jax.experimental import pallas as pl
from jax.experimental.pallas import tpu as pltpu

F32 = jnp.float32
BF16 = jnp.bfloat16

RMS_EPS = 1e-6
HEAD_DIM = 128
SSM_GROUP = 16
SSM_STATE = 64
PAIR_CH = 2 * SSM_GROUP
PAIR_ST = 2 * SSM_STATE
MOE_GROUPS = 4
EXPERTS_PER_GROUP = 4
N_EXPERTS = MOE_GROUPS * EXPERTS_PER_GROUP
ROUTER_LANES = 128
N_MOD = 6
ROW_TILE = 256
SUBLANES = 8
VMEM_LIMIT_BYTES = 56 * 1024 * 1024
ATTN_TQ = 256
ATTN_TK = 256
MOE_ROW_TILES = 8
PROMPT_TILES_PER_BODY = 8
LOG2_E = 1.4426950408889634
Q_SCALE_LOG2 = HEAD_DIM ** -0.5 * LOG2_E


def _params(*sem):
    return pltpu.CompilerParams(dimension_semantics=sem, vmem_limit_bytes=VMEM_LIMIT_BYTES)


def _dot(a, b):
    return jnp.dot(a, b, preferred_element_type=F32)


def _dot_nt(a, b):
    return lax.dot_general(a, b, (((1,), (1,)), ((), ())), preferred_element_type=F32)


def _dot_exact(a, b):
    return jnp.dot(a, b, preferred_element_type=F32, precision=lax.Precision.HIGHEST)


def _adaln_kernel(c_ref, w_ref, b_ref, o_ref):
    c = c_ref[...]
    a = (c * jax.nn.sigmoid(c)).astype(BF16)
    o_ref[...] = _dot(a, w_ref[...].astype(BF16)) + b_ref[...]


def _adaln(c_all, w_ada, b_ada):
    n_layers, d, n_out = w_ada.shape
    r = c_all.shape[0]
    tn = 1024
    return pl.pallas_call(
        _adaln_kernel,
        grid=(n_layers, n_out // tn),
        in_specs=[pl.BlockSpec((r, d), lambda l, j: (0, 0)),
                  pl.BlockSpec((None, d, tn), lambda l, j: (l, 0, j)),
                  pl.BlockSpec((None, 1, tn), lambda l, j: (l, 0, j))],
        out_specs=pl.BlockSpec((None, r, tn), lambda l, j: (l, 0, j)),
        out_shape=jax.ShapeDtypeStruct((n_layers, r, n_out), F32),
        compiler_params=_params("arbitrary", "arbitrary"),
        name="adaln",
    )(c_all, w_ada, b_ada.reshape(n_layers, 1, n_out))


def _rows_mod(v, m):
    tm, d = v.shape
    return v.reshape(tm // SUBLANES, SUBLANES, d), m[:, None, :]


def _modulated_norm(x, g, sc, sh):
    ms = jnp.mean(x * x, axis=-1, keepdims=True)
    y = x * lax.rsqrt(ms + RMS_EPS) * g
    y3, sc3 = _rows_mod(y, sc)
    h = y3 * (1.0 + sc3) + sh[:, None, :]
    return h.reshape(x.shape)


def _mod_spec(n_prompt_tiles, d, which):
    return pl.BlockSpec((None, ROW_TILE // SUBLANES, d),
                        lambda i: (jnp.where(i < n_prompt_tiles, 0, 1), 0, which))


def _norm_mod_kernel(x_ref, g_ref, sc_ref, sh_ref, o_ref):
    o_ref[...] = _modulated_norm(x_ref[...], g_ref[...], sc_ref[...], sh_ref[...]).astype(BF16)


def _res_norm_mod_kernel(x1_ref, f_ref, g2_ref, g_ref, sc_ref, sh_ref, x_ref, h_ref):
    f3, g23 = _rows_mod(f_ref[...], g2_ref[...])
    x = x1_ref[...] + (f3 * g23).reshape(x_ref.shape)
    x_ref[...] = x
    h_ref[...] = _modulated_norm(x, g_ref[...], sc_ref[...], sh_ref[...]).astype(BF16)


def _res_norm_mod(x1, ffn, mod3_prev, g, mod3, n_prompt_tiles):
    n, d = x1.shape
    row = pl.BlockSpec((ROW_TILE, d), lambda i: (i, 0))
    return pl.pallas_call(
        _res_norm_mod_kernel,
        grid=(n // ROW_TILE,),
        in_specs=[row, row, _mod_spec(n_prompt_tiles, d, 5),
                  pl.BlockSpec((1, d), lambda i: (0, 0)),
                  _mod_spec(n_prompt_tiles, d, 1), _mod_spec(n_prompt_tiles, d, 0)],
        out_specs=[row, row],
        out_shape=[jax.ShapeDtypeStruct((n, d), F32), jax.ShapeDtypeStruct((n, d), BF16)],
        compiler_params=_params("arbitrary"),
        name="res_norm_mod",
    )(x1, ffn, mod3_prev, g.reshape(1, d), mod3, mod3)


def _norm_mod(x, g, mod3, n_prompt_tiles, sc_idx, sh_idx):
    n, d = x.shape
    return pl.pallas_call(
        _norm_mod_kernel,
        grid=(n // ROW_TILE,),
        in_specs=[pl.BlockSpec((ROW_TILE, d), lambda i: (i, 0)),
                  pl.BlockSpec((1, d), lambda i: (0, 0)),
                  _mod_spec(n_prompt_tiles, d, sc_idx),
                  _mod_spec(n_prompt_tiles, d, sh_idx)],
        out_specs=pl.BlockSpec((ROW_TILE, d), lambda i: (i, 0)),
        out_shape=jax.ShapeDtypeStruct((n, d), BF16),
        compiler_params=_params("arbitrary"),
        name="norm_mod",
    )(x, g.reshape(1, d), mod3, mod3)


def _store_token_major(ref, val):
    tm = val.shape[0]
    n_heads = val.shape[1] // HEAD_DIM
    for hh in range(n_heads):
        ref[pl.ds(hh, tm, stride=n_heads), :] = val[:, hh * HEAD_DIM:(hh + 1) * HEAD_DIM]


def _split_specs(n_prompt_tiles, block, lead=()):
    zeros = (0,) * (len(block) - 1)
    nones = (None,) * len(lead)
    return [pl.BlockSpec(nones + block, lambda i: lead + (jnp.minimum(i, n_prompt_tiles - 1),) + zeros),
            pl.BlockSpec(nones + block, lambda i: lead + (0,) + zeros)]


def _store_split(n_prompt_tiles, prompt_ref, sample_ref, store):
    i = pl.program_id(0)

    @pl.when(i < n_prompt_tiles)
    def _():
        store(prompt_ref)

    @pl.when(i >= n_prompt_tiles)
    def _():
        store(sample_ref)


def _proj_uv_kernel(h_ref, wu_ref, wv_ref, *rest, n_prompt_tiles):
    u_ref, ub_ref, vb_ref, vp_ref, vs_ref = rest[-5:]
    h = h_ref[...]
    u = _dot(h, wu_ref[...])
    u_ref[...] = u
    ub_ref[...] = u.astype(BF16)
    v = _dot(h, wv_ref[...])
    vb_ref[...] = v.astype(BF16)
    _store_split(n_prompt_tiles, vp_ref, vs_ref, lambda ref: _store_token_major(ref, v))


def _head_norm(acc, g):
    outs = []
    for hh in range(acc.shape[1] // HEAD_DIM):
        blk = acc[:, hh * HEAD_DIM:(hh + 1) * HEAD_DIM]
        ms = jnp.mean(blk * blk, axis=-1, keepdims=True)
        outs.append(blk * lax.rsqrt(ms + RMS_EPS) * g)
    return jnp.concatenate(outs, axis=1)


def _proj_qk_kernel(h_ref, wq_ref, wk_ref, gq_ref, gk_ref, *rest, n_prompt_tiles):
    qb_ref, kb_ref, kp_ref, ks_ref = rest[-4:]
    h = h_ref[...]
    qb_ref[...] = (_head_norm(_dot(h, wq_ref[...]), gq_ref[...]) * Q_SCALE_LOG2).astype(BF16)
    k = _head_norm(_dot(h, wk_ref[...]), gk_ref[...])
    kb_ref[...] = k.astype(BF16)
    _store_split(n_prompt_tiles, kp_ref, ks_ref, lambda ref: _store_token_major(ref, k))


def _proj_gate_kernel(h_ref, w_ref, o_ref):
    o_ref[...] = jax.nn.sigmoid(_dot(h_ref[...], w_ref[...]))


def _in_proj(h, w_in_b, layer, q_g, k_g, width, n_prompt_tiles, kv_prev):
    n, d = h.shape
    nt = n // ROW_TILE
    n_layers = w_in_b.shape[0]
    row = pl.BlockSpec((ROW_TILE, d), lambda i: (i, 0))
    out = pl.BlockSpec((ROW_TILE, width), lambda i: (i, 0))

    def wcol(j):
        return pl.BlockSpec((None, d, width), lambda i: (layer, 0, j))

    sds = lambda dt: jax.ShapeDtypeStruct((n, width), dt)
    n_heads = width // HEAD_DIM
    tok_rows = ROW_TILE * n_heads
    tok = _split_specs(n_prompt_tiles, (tok_rows, HEAD_DIM), (layer,))
    tok_sds = [jax.ShapeDtypeStruct((n_layers, n_prompt_tiles * tok_rows, HEAD_DIM), F32),
               jax.ShapeDtypeStruct((n_layers, (nt - n_prompt_tiles) * tok_rows, HEAD_DIM), F32)]
    assert nt - n_prompt_tiles == 1
    if kv_prev is None:
        prev_k, prev_v, prev_specs = (), (), []
    else:
        prev_k, prev_v = kv_prev[:2], kv_prev[2:]
        prev_specs = [pl.BlockSpec(memory_space=pl.ANY)] * 2
    n_prev = len(prev_specs)
    u, u_b, v_b, v_p, v_s = pl.pallas_call(
        functools.partial(_proj_uv_kernel, n_prompt_tiles=n_prompt_tiles), grid=(nt,),
        in_specs=[row, wcol(0), wcol(3)] + prev_specs, out_specs=[out, out, out] + tok,
        out_shape=[sds(F32), sds(BF16), sds(BF16)] + tok_sds,
        input_output_aliases={3 + j: 3 + j for j in range(n_prev)},
        compiler_params=_params("arbitrary"), name="proj_uv",
    )(h, w_in_b, w_in_b, *prev_v)
    gspec = pl.BlockSpec((1, HEAD_DIM), lambda i: (0, 0))
    q_b, k_b, k_p, k_s = pl.pallas_call(
        functools.partial(_proj_qk_kernel, n_prompt_tiles=n_prompt_tiles), grid=(nt,),
        in_specs=[row, wcol(1), wcol(2), gspec, gspec] + prev_specs, out_specs=[out, out] + tok,
        out_shape=[sds(BF16), sds(BF16)] + tok_sds,
        input_output_aliases={5 + j: 2 + j for j in range(n_prev)},
        compiler_params=_params("arbitrary"), name="proj_qk",
    )(h, w_in_b, w_in_b, q_g.reshape(1, HEAD_DIM), k_g.reshape(1, HEAD_DIM), *prev_k)
    n_gate = (w_in_b.shape[2] - 4 * width) // width
    gates = pl.pallas_call(
        _proj_gate_kernel, grid=(n_gate, nt),
        in_specs=[pl.BlockSpec((ROW_TILE, d), lambda j, i: (i, 0)),
                  pl.BlockSpec((None, d, width), lambda j, i: (layer, 0, 4 + j))],
        out_specs=pl.BlockSpec((ROW_TILE, width), lambda j, i: (i, j)),
        out_shape=jax.ShapeDtypeStruct((n, n_gate * width), F32),
        compiler_params=_params("arbitrary", "arbitrary"), name="proj_gate",
    )(h, w_in_b)
    return u, u_b, q_b, k_b, v_b, gates, (k_p, k_s, v_p, v_s)


def _cmul(ar, ai, br, bi):
    return ar * br - ai * bi, ar * bi + ai * br


def _discretise(a_re, a_im, log_dt):
    dt = jnp.exp(log_dt)
    mag = jnp.exp(dt * a_re)
    ab_re = mag * jnp.cos(dt * a_im)
    ab_im = mag * jnp.sin(dt * a_im)
    den = a_re * a_re + a_im * a_im
    f_re = ((ab_re - 1.0) * a_re + ab_im * a_im) / den
    f_im = (ab_im * a_re - (ab_re - 1.0) * a_im) / den
    return dt, f_re, f_im


def _abar_pow(a_re, a_im, dt, k):
    mag = jnp.exp(k * dt * a_re)
    th = k * dt * a_im
    return mag * jnp.cos(th), mag * jnp.sin(th)


def _ssm_prep_kernel(as_re_ref, as_im_ref, ldts_ref, al_re_ref, al_im_ref, ldtl_ref,
                     bt_re_ref, bt_im_ref, btr_re_ref, btr_im_ref, c_re_ref, c_im_ref,
                     ct_re_ref, ct_im_ref,
                     m_ref, e_re_ref, e_im_ref, f_re_ref, f_im_ref, p_re_ref, p_im_ref, *, chunk):
    width = chunk * PAIR_CH
    as_re, as_im = as_re_ref[...], as_im_ref[...]
    dts, fs_re, fs_im = _discretise(as_re, as_im, ldts_ref[...])
    al_re, al_im = al_re_ref[...], al_im_ref[...]
    dtl, fl_re, fl_im = _discretise(al_re, al_im, ldtl_ref[...])

    lane_blk = lax.shift_right_logical(lax.broadcasted_iota(jnp.int32, (1, width), 1), 5)
    assert PAIR_CH == 32

    k_rev = (chunk - 1 - lane_blk).astype(F32)
    pk_re, pk_im = _abar_pow(as_re, as_im, dts, k_rev)
    bb_re, bb_im = _cmul(fs_re, fs_im, bt_re_ref[...], bt_im_ref[...])
    w_re, w_im = _cmul(pk_re, pk_im, bb_re, bb_im)
    kr = _dot_exact(c_re_ref[...], w_re) - _dot_exact(c_im_ref[...], w_im)
    lane = lax.broadcasted_iota(jnp.int32, (PAIR_CH, width), 1)
    for t in range(chunk):
        shift = (chunk - 1 - t) * PAIR_CH
        rolled = kr if shift == 0 else pltpu.roll(kr, width - shift, axis=1)
        m_ref[t * PAIR_CH:(t + 1) * PAIR_CH, :] = jnp.where(lane < (t + 1) * PAIR_CH, rolled, 0.0)

    bbt_re, bbt_im = _cmul(fl_re, fl_im, btr_re_ref[...], btr_im_ref[...])
    for tau in range(chunk):
        q_re, q_im = _abar_pow(al_re, al_im, dtl, float(chunk - 1 - tau))
        e_re, e_im = _cmul(q_re, q_im, bbt_re, bbt_im)
        e_re_ref[tau * PAIR_CH:(tau + 1) * PAIR_CH, :] = e_re
        e_im_ref[tau * PAIR_CH:(tau + 1) * PAIR_CH, :] = e_im

    k_f = (lane_blk + 1).astype(F32)
    pf_re, pf_im = _abar_pow(as_re, as_im, dts, k_f)
    ca_re, ca_im = _cmul(ct_re_ref[...], ct_im_ref[...], pf_re, pf_im)
    f_re_ref[...] = ca_re
    f_im_ref[...] = -ca_im

    pt_re, pt_im = _abar_pow(al_re, al_im, dtl, float(chunk))
    ph_re, ph_im = _abar_pow(al_re, al_im, dtl, float(chunk // 2))
    p_re_ref[...] = jnp.concatenate([pt_re, ph_re], axis=0)
    p_im_ref[...] = jnp.concatenate([pt_im, ph_im], axis=0)


def _ssm_layouts(a_re, a_im, log_dt, b_re, b_im, c_re, c_im, chunk):
    n_groups = a_re.shape[0]
    n_pairs = n_groups // 2
    eye = jnp.eye(2, dtype=F32)

    def sub(x):
        return x.reshape(n_pairs, PAIR_ST, 1)

    def lan(x):
        return x.reshape(n_pairs, 1, PAIR_ST)

    ldt = jnp.broadcast_to(log_dt[:, None], (n_groups, SSM_STATE))

    def b_tiled(b):
        bp = b.reshape(n_pairs, 2, SSM_STATE, SSM_GROUP)
        bd = jnp.einsum('ngph,gk->ngpkh', bp, eye).reshape(n_pairs, PAIR_ST, PAIR_CH)
        return jnp.tile(bd, (1, 1, chunk))

    def b_rows(b):
        bp = b.reshape(n_pairs, 2, SSM_STATE, SSM_GROUP)
        return jnp.einsum('ngph,gk->nghkp', bp, eye).reshape(n_pairs, PAIR_CH, PAIR_ST)

    def c_rows(c):
        cp = c.reshape(n_pairs, 2, SSM_GROUP, SSM_STATE)
        return jnp.einsum('nghp,gk->nghkp', cp, eye).reshape(n_pairs, PAIR_CH, PAIR_ST)

    def c_tiled(c):
        cp = c.reshape(n_pairs, 2, SSM_GROUP, SSM_STATE)
        cd = jnp.einsum('nghp,gk->nkpgh', cp, eye).reshape(n_pairs, PAIR_ST, PAIR_CH)
        return jnp.tile(cd, (1, 1, chunk))

    return (sub(a_re), sub(a_im), sub(ldt), lan(a_re), lan(a_im), lan(ldt),
            b_tiled(b_re), b_tiled(b_im), b_rows(b_re), b_rows(b_im),
            c_rows(c_re), c_rows(c_im), c_tiled(c_re), c_tiled(c_im))


def _ssm_prep(a_re, a_im, log_dt, b_re, b_im, c_re, c_im, chunk):
    n_pairs = a_re.shape[0] // 2
    width = chunk * PAIR_CH
    args = _ssm_layouts(a_re, a_im, log_dt, b_re, b_im, c_re, c_im, chunk)

    def spec(r, c):
        return pl.BlockSpec((None, r, c), lambda i: (i, 0, 0))

    sub, lan = spec(PAIR_ST, 1), spec(1, PAIR_ST)
    wide, rows = spec(PAIR_ST, width), spec(PAIR_CH, PAIR_ST)
    sds = lambda r, c: jax.ShapeDtypeStruct((n_pairs, r, c), F32)
    return pl.pallas_call(
        functools.partial(_ssm_prep_kernel, chunk=chunk),
        grid=(n_pairs,),
        in_specs=[sub, sub, sub, lan, lan, lan, wide, wide, rows, rows, rows, rows, wide, wide],
        out_specs=[spec(width, width), spec(width, PAIR_ST), spec(width, PAIR_ST),
                   wide, wide, spec(2, PAIR_ST), spec(2, PAIR_ST)],
        out_shape=[sds(width, width), sds(width, PAIR_ST), sds(width, PAIR_ST),
                   sds(PAIR_ST, width), sds(PAIR_ST, width), sds(2, PAIR_ST), sds(2, PAIR_ST)],
        compiler_params=_params("arbitrary"),
        name="ssm_prep",
    )(*args)


def _ssm_loc_kernel(u_ref, e_re_ref, e_im_ref, l_re_ref, l_im_ref):
    ub = u_ref[...]
    l_re_ref[...] = _dot(ub, e_re_ref[...].astype(BF16))
    l_im_ref[...] = _dot(ub, e_im_ref[...].astype(BF16))


def _ssm_loc(u_pairs, e_re, e_im):
    n_pairs, rows, width = u_pairs.shape
    st = pl.BlockSpec((rows, PAIR_ST), lambda i: (0, i))
    sds = jax.ShapeDtypeStruct((rows, n_pairs * PAIR_ST), F32)
    espec = pl.BlockSpec((None, width, PAIR_ST), lambda i: (i, 0, 0))
    return pl.pallas_call(
        _ssm_loc_kernel, grid=(n_pairs,),
        in_specs=[pl.BlockSpec((None, rows, width), lambda i: (i, 0, 0)), espec, espec],
        out_specs=[st, st], out_shape=[sds, sds],
        compiler_params=_params("arbitrary"), name="ssm_loc",
    )(u_pairs, e_re, e_im)


def _ssm_scan_kernel(l_re_ref, l_im_ref, p_re_ref, p_im_ref, s_re_ref, s_im_ref):
    p_re, p_im = p_re_ref[...], p_im_ref[...]
    n_chunks = l_re_ref.shape[0]

    def body(c, s):
        s_re, s_im = s
        s_re_ref[pl.ds(c, 1), :] = s_re
        s_im_ref[pl.ds(c, 1), :] = s_im
        n_re, n_im = _cmul(p_re, p_im, s_re, s_im)
        return n_re + l_re_ref[pl.ds(c, 1), :], n_im + l_im_ref[pl.ds(c, 1), :]

    zero = jnp.zeros_like(p_re)
    lax.fori_loop(0, n_chunks, body, (zero, zero))


def _ssm_scan(loc_re, loc_im, p_re, p_im):
    rows, n_state = loc_re.shape
    tc = 1024
    blk = pl.BlockSpec((rows, tc), lambda j: (0, j))
    par = pl.BlockSpec((1, tc), lambda j: (0, j))
    sds = jax.ShapeDtypeStruct((rows, n_state), F32)
    return pl.pallas_call(
        _ssm_scan_kernel, grid=(n_state // tc,),
        in_specs=[blk, blk, par, par], out_specs=[blk, blk], out_shape=[sds, sds],
        compiler_params=_params("arbitrary"), name="ssm_scan",
    )(loc_re, loc_im, p_re[:, 0].reshape(1, n_state), p_im[:, 0].reshape(1, n_state))


def _gelu_tanh(x):
    c = math.sqrt(2.0 / math.pi)
    return 0.5 * x * (1.0 + jnp.tanh(c * (x + 0.044715 * (x * x * x))))


def _ssm_out_kernel(u_ref, s_re_ref, s_im_ref, m_ref, e_re_ref, e_im_ref, f_re_ref, f_im_ref,
                    p_re_ref, p_im_ref, y_ref, n_re_ref, n_im_ref):
    ub = u_ref[...]
    s_re, s_im = s_re_ref[...], s_im_ref[...]
    y = (_dot_nt(ub, m_ref[...].astype(BF16))
         + _dot(s_re.astype(BF16), f_re_ref[...].astype(BF16))
         + _dot(s_im.astype(BF16), f_im_ref[...].astype(BF16)))
    y_ref[...] = y.astype(BF16)
    a_re, a_im = _cmul(p_re_ref[...], p_im_ref[...], s_re, s_im)
    n_re_ref[...] = a_re + _dot(ub, e_re_ref[...].astype(BF16))
    n_im_ref[...] = a_im + _dot(ub, e_im_ref[...].astype(BF16))


def _ssm_out(u_pairs, s_re, s_im, prep, half):
    n_pairs, rows, width = u_pairs.shape
    m, e_re, e_im, f_re, f_im, p_re, p_im = prep
    st = pl.BlockSpec((rows, PAIR_ST), lambda i: (0, i))
    per = lambda r, c: pl.BlockSpec((None, r, c), lambda i: (i, 0, 0))
    e_spec = pl.BlockSpec((None, width, PAIR_ST), lambda i: (i, 1 if half else 0, 0))
    p_spec = pl.BlockSpec((None, 1, PAIR_ST), lambda i: (i, 0, 0))
    sds = jax.ShapeDtypeStruct((rows, n_pairs * PAIR_ST), F32)
    pick = slice(1, 2) if half else slice(0, 1)
    return pl.pallas_call(
        _ssm_out_kernel, grid=(n_pairs,),
        in_specs=[per(rows, width), st, st, per(width, width), e_spec, e_spec,
                  per(PAIR_ST, width), per(PAIR_ST, width), p_spec, p_spec],
        out_specs=[per(rows, width), st, st],
        out_shape=[jax.ShapeDtypeStruct((n_pairs, rows, width), BF16), sds, sds],
        compiler_params=_params("arbitrary"), name="ssm_out",
    )(u_pairs, s_re, s_im, m, e_re, e_im, f_re, f_im, p_re[:, pick], p_im[:, pick])


def _to_pairs(u, rows, chunk):
    n_pairs = u.shape[1] // PAIR_CH
    return (u.reshape(rows, chunk, n_pairs, PAIR_CH).transpose(2, 0, 1, 3)
            .reshape(n_pairs, rows, chunk * PAIR_CH))


def _from_pairs(y, rows, chunk):
    n_pairs = y.shape[0]
    return (y.reshape(n_pairs, rows, chunk, PAIR_CH).transpose(1, 2, 0, 3)
            .reshape(rows * chunk, n_pairs * PAIR_CH))


def _glu_kernel(ys_ref, u_ref, d_ref, w_ref, b_ref, o_ref):
    y = _gelu_tanh(ys_ref[...].astype(F32) + d_ref[...] * u_ref[...])
    t = _dot(y.astype(BF16), w_ref[...]) + b_ref[...]
    o_ref[...] = (y * jax.nn.sigmoid(t)).astype(BF16)


def _glu(y_ssm, u, d_skip, w_b, layer, b):
    n, c = u.shape
    row = pl.BlockSpec((ROW_TILE, c), lambda i: (i, 0))
    vec = pl.BlockSpec((1, c), lambda i: (0, 0))
    return pl.pallas_call(
        _glu_kernel, grid=(n // ROW_TILE,),
        in_specs=[row, row, vec, pl.BlockSpec((None, c, c), lambda i: (layer, 0, 0)), vec],
        out_specs=row,
        out_shape=jax.ShapeDtypeStruct((n, c), BF16),
        compiler_params=_params("arbitrary"), name="glu",
    )(y_ssm, u, d_skip.reshape(1, c), w_b, b.reshape(1, c))


def _suffix_matrix(tk):
    j = jnp.arange(tk)[:, None]
    s = jnp.arange(tk + HEAD_DIM)[None, :]
    return ((j > s) | (s >= tk)).astype(BF16)


def _stick_weights(z2, carry, tri, mask, split):
    tk = z2.shape[1]
    neg_abs = lax.bitcast_convert_type(
        lax.bitcast_convert_type(z2, jnp.uint32) | jnp.uint32(0x80000000), F32)
    sp = jnp.maximum(z2, 0.0) + jnp.log2(1.0 + jnp.exp2(neg_abs))
    spm = sp if mask is None else jnp.where(mask, sp, 0.0)
    hi = spm.astype(BF16)
    tot = _dot(hi, tri)
    if split:
        tot = tot + _dot((spm - hi.astype(F32)).astype(BF16), tri)
    w = jnp.exp2((z2 - sp) - tot[:, :tk] - jnp.concatenate([carry] * (tk // HEAD_DIM), axis=1))
    if mask is not None:
        w = jnp.where(mask, w, 0.0)
    return w, carry + tot[:, tk:]


def _attn_prompt_kernel(bias_ref, q_ref, k_ref, v_ref, tri_ref, o_ref, carry_ref, acc_ref):
    hh = pl.program_id(0)
    i = pl.program_id(1)
    tq, tk = ATTN_TQ, ATTN_TK
    bias = bias_ref[hh]
    q = q_ref[...]
    tri = tri_ref[...]

    def tile(j, carry, acc, mask):
        k = k_ref[pl.ds(pl.multiple_of(j * tk, tk), tk), :]
        v = v_ref[pl.ds(pl.multiple_of(j * tk, tk), tk), :]
        w, carry = _stick_weights(_dot_nt(q, k) + bias, carry, tri, mask, split=False)
        return carry, acc + _dot(w.astype(BF16), v)

    def run(j0, count):
        base = pl.multiple_of((j0 - (count - 1)) * tk, tk)
        z = _dot_nt(q, k_ref[pl.ds(base, count * tk), :]) + bias
        z2 = jnp.concatenate([z[:, a * tk:(a + 1) * tk] for a in range(count)], axis=0)
        neg_abs = lax.bitcast_convert_type(
            lax.bitcast_convert_type(z2, jnp.uint32) | jnp.uint32(0x80000000), F32)
        sp = jnp.maximum(z2, 0.0) + jnp.log2(1.0 + jnp.exp2(neg_abs))
        tot = _dot(sp.astype(BF16), tri)
        carry = carry_ref[...]
        carries = [None] * count
        for a in reversed(range(count)):
            carries[a] = jnp.concatenate([carry] * (tk // HEAD_DIM), axis=1)
            carry = carry + tot[a * tq:(a + 1) * tq, tk:]
        carry_ref[...] = carry
        w = jnp.exp2((z2 - sp) - tot[:, :tk] - jnp.concatenate(carries, axis=0))
        w = jnp.concatenate([w[a * tq:(a + 1) * tq] for a in range(count)], axis=1).astype(BF16)
        acc_ref[...] += _dot(w, v_ref[pl.ds(base, count * tk), :])

    row = lax.broadcasted_iota(jnp.int32, (tq, tk), 0)
    col = lax.broadcasted_iota(jnp.int32, (tq, tk), 1)
    zero = jnp.zeros((tq, HEAD_DIM), F32)
    carry_ref[...], acc_ref[...] = tile(i, zero, zero, col < row)

    group = PROMPT_TILES_PER_BODY
    n_groups = lax.div(i, group)

    def body(n, c):
        run(i - 1 - group * n, group)
        return c

    lax.fori_loop(0, n_groups, body, 0)
    left = i - group * n_groups
    part = group // 2
    while part >= 1:
        @pl.when((left & part) != 0)
        def _(part=part, left=left):
            run((left & (2 * part - 1)) - 1, part)
        part //= 2
    o_ref[...] = acc_ref[...].astype(BF16)


def _attn_prompt(q_b, k_b, v_b, bias2, seq):
    n_heads = q_b.shape[1] // HEAD_DIM
    kv = pl.BlockSpec((seq, HEAD_DIM), lambda h, i: (0, h))
    return pl.pallas_call(
        _attn_prompt_kernel,
        grid=(n_heads, seq // ATTN_TQ),
        in_specs=[pl.BlockSpec(memory_space=pltpu.SMEM),
                  pl.BlockSpec((ATTN_TQ, HEAD_DIM), lambda h, i: (i, h)),
                  kv, kv,
                  pl.BlockSpec((ATTN_TK, ATTN_TK + HEAD_DIM), lambda h, i: (0, 0))],
        out_specs=pl.BlockSpec((ATTN_TQ, HEAD_DIM), lambda h, i: (i, h)),
        out_shape=jax.ShapeDtypeStruct((seq, n_heads * HEAD_DIM), BF16),
        scratch_shapes=[pltpu.VMEM((ATTN_TQ, HEAD_DIM), F32), pltpu.VMEM((ATTN_TQ, HEAD_DIM), F32)],
        compiler_params=_params("arbitrary", "arbitrary"), name="attn_prompt",
    )(bias2, q_b, k_b, v_b, _suffix_matrix(ATTN_TK))


def _attn_sample_kernel(pt_ref, q_ref, kn_ref, vn_ref, *rest, n_heads, n_q, pages_per_step):
    kc_refs = rest[:pages_per_step]
    vc_refs = rest[pages_per_step:2 * pages_per_step]
    bias_ref, tri_ref, o_ref, carry_ref, acc_ref = rest[2 * pages_per_step:]
    p = pl.program_id(1)
    page = kn_ref.shape[0]
    rows = n_heads * n_q
    tri = tri_ref[...]
    bias = bias_ref[...]

    def head_q(hh):
        return q_ref[:, hh * HEAD_DIM:(hh + 1) * HEAD_DIM]

    def sweep(k_of, v_of, mask, carry, acc):
        z = jnp.concatenate([_dot_nt(head_q(hh), k_of(hh)) for hh in range(n_heads)], axis=0)
        w, carry = _stick_weights(z + bias, carry, tri, mask, split=True)
        return carry, acc + jnp.concatenate(
            [_dot(w[hh * n_q:(hh + 1) * n_q].astype(BF16), v_of(hh)) for hh in range(n_heads)], axis=0)

    @pl.when(p == 0)
    def _():
        q_idx = lax.broadcasted_iota(jnp.int32, (rows, page), 0) & (n_q - 1)
        key = lax.broadcasted_iota(jnp.int32, (rows, page), 1)
        zero = jnp.zeros((rows, HEAD_DIM), F32)
        carry_ref[...], acc_ref[...] = sweep(
            lambda hh: kn_ref[:, hh * HEAD_DIM:(hh + 1) * HEAD_DIM],
            lambda hh: vn_ref[:, hh * HEAD_DIM:(hh + 1) * HEAD_DIM], key < q_idx, zero, zero)

    def head_rows(ref, hh):
        return ref[pl.ds(hh, page, stride=n_heads), :].astype(BF16)

    pps = pages_per_step
    z = jnp.concatenate(
        [_dot_nt(head_q(hh), jnp.concatenate([head_rows(r, hh) for r in kc_refs], axis=0))
         for hh in range(n_heads)], axis=0)
    z2 = jnp.concatenate([z[:, r * page:(r + 1) * page] + bias for r in range(pps)], axis=0)
    neg_abs = lax.bitcast_convert_type(
        lax.bitcast_convert_type(z2, jnp.uint32) | jnp.uint32(0x80000000), F32)
    sp = jnp.maximum(z2, 0.0) + jnp.log2(1.0 + jnp.exp2(neg_abs))
    hi = sp.astype(BF16)
    tot = _dot(hi, tri) + _dot((sp - hi.astype(F32)).astype(BF16), tri)
    carry = carry_ref[...]
    carries = []
    for r in range(pps):
        carries.append(carry)
        carry = carry + tot[r * rows:(r + 1) * rows, page:]
    carry_ref[...] = carry
    w = jnp.exp2((z2 - sp) - tot[:, :page] - jnp.concatenate(carries, axis=0))
    acc_ref[...] += jnp.concatenate(
        [_dot(jnp.concatenate([w[r * rows + hh * n_q:r * rows + (hh + 1) * n_q] for r in range(pps)],
                              axis=1).astype(BF16),
              jnp.concatenate([head_rows(r, hh) for r in vc_refs], axis=0))
         for hh in range(n_heads)], axis=0)

    @pl.when(p == pl.num_programs(1) - 1)
    def _():
        acc = acc_ref[...]
        for hh in range(n_heads):
            o_ref[:, hh * HEAD_DIM:(hh + 1) * HEAD_DIM] = acc[hh * n_q:(hh + 1) * n_q]


SAMPLE_PAGES_PER_STEP = 16


def _attn_sample(q_s, k_new, v_new, cache_k, cache_v, layer, page_table, bias2):
    n_seq, n_q, width = q_s.shape
    n_heads = width // HEAD_DIM
    n_layers, n_pool, page = cache_k.shape[:3]
    n_pages = page_table.shape[1]
    pps = SAMPLE_PAGES_PER_STEP
    assert n_q & (n_q - 1) == 0 and n_pages % pps == 0
    rows = n_heads * n_q
    bias_rows = jnp.repeat(bias2, n_q).reshape(rows, 1)
    flat = (n_layers, n_pool, page * n_heads, HEAD_DIM)

    def cache(r):
        return pl.BlockSpec((None, None, page * n_heads, HEAD_DIM),
                            lambda b, p, pt: (layer, pt[b * n_pages + (n_pages - 1 - (p * pps + r))], 0, 0))

    new = pl.BlockSpec((None, page, width), lambda b, p, pt: (b, 0, 0))
    caches = [cache(r) for r in range(pps)]
    return pl.pallas_call(
        functools.partial(_attn_sample_kernel, n_heads=n_heads, n_q=n_q, pages_per_step=pps),
        grid_spec=pltpu.PrefetchScalarGridSpec(
            num_scalar_prefetch=1,
            grid=(n_seq, n_pages // pps),
            in_specs=[pl.BlockSpec((None, n_q, width), lambda b, p, pt: (b, 0, 0)),
                      new, new, *caches, *caches,
                      pl.BlockSpec((rows, 1), lambda b, p, pt: (0, 0)),
                      pl.BlockSpec((page, page + HEAD_DIM), lambda b, p, pt: (0, 0))],
            out_specs=pl.BlockSpec((None, n_q, width), lambda b, p, pt: (b, 0, 0)),
            scratch_shapes=[pltpu.VMEM((rows, HEAD_DIM), F32), pltpu.VMEM((rows, HEAD_DIM), F32)]),
        out_shape=jax.ShapeDtypeStruct((n_seq, n_q, width), F32),
        compiler_params=_params("arbitrary", "arbitrary"), name="attn_sample",
    )(page_table.reshape(-1), q_s, k_new, v_new,
      *([cache_k.reshape(flat)] * pps), *([cache_v.reshape(flat)] * pps), bias_rows, _suffix_matrix(page))


def _merge_kernel(a_ref, b_ref, ga_ref, gb_ref, wa_ref, wb_ref, o_ref):
    m = ga_ref[...] * _dot(a_ref[...], wa_ref[...]) + gb_ref[...] * _dot(b_ref[...], wb_ref[...])
    o_ref[...] = m.astype(BF16)


def _merge(a, b, gates, wa_b, wb_b, layer):
    n, c = a.shape
    d = wa_b.shape[2]
    row = pl.BlockSpec((ROW_TILE, c), lambda i: (i, 0))
    w = pl.BlockSpec((None, c, d), lambda i: (layer, 0, 0))
    return pl.pallas_call(
        _merge_kernel, grid=(n // ROW_TILE,),
        in_specs=[row, row,
                  pl.BlockSpec((ROW_TILE, d), lambda i: (i, 0)),
                  pl.BlockSpec((ROW_TILE, d), lambda i: (i, 1)), w, w],
        out_specs=pl.BlockSpec((ROW_TILE, d), lambda i: (i, 0)),
        out_shape=jax.ShapeDtypeStruct((n, d), BF16),
        compiler_params=_params("arbitrary"), name="merge",
    )(a, b, gates, gates, wa_b, wb_b)


def _route(logits):
    lane_i = lax.broadcasted_iota(jnp.int32, logits.shape, 1)
    lane = lane_i.astype(F32)
    neg = jnp.float32(-jnp.inf)
    big = jnp.float32(ROUTER_LANES)
    is_group = (lane_i >= N_EXPERTS) & (lane_i < N_EXPERTS + MOE_GROUPS)
    g_log = jnp.where(is_group, logits, neg)
    g_max = jnp.max(g_log, axis=-1, keepdims=True)
    g_idx = jnp.min(jnp.where(g_log == g_max, lane, big), axis=-1, keepdims=True) - N_EXPERTS
    g_prob = 1.0 / jnp.sum(jnp.where(is_group, jnp.exp(g_log - g_max), 0.0), axis=-1, keepdims=True)
    lane_group = lax.shift_right_logical(lane_i, 2).astype(F32)
    in_group = (lane_i < N_EXPERTS) & (lane_group == g_idx)
    assert EXPERTS_PER_GROUP == 4
    e_log = jnp.where(in_group, logits, neg)
    v1 = jnp.max(e_log, axis=-1, keepdims=True)
    i1 = jnp.min(jnp.where(e_log == v1, lane, big), axis=-1, keepdims=True)
    e_log2 = jnp.where(lane == i1, neg, e_log)
    v2 = jnp.max(e_log2, axis=-1, keepdims=True)
    i2 = jnp.min(jnp.where(e_log2 == v2, lane, big), axis=-1, keepdims=True)
    e2 = jnp.exp(v2 - v1)
    den = 1.0 + e2
    w1 = g_prob / den
    w2 = g_prob * e2 / den
    return jnp.where(lane == i1, w1, jnp.where(lane == i2, w2, 0.0))


def _out_kernel(m_ref, x_ref, w_ref, g1_ref, gn_ref, sc_ref, sh_ref, wr_hi_ref, wr_lo_ref, br_ref,
                x1_ref, h2_ref, gates_ref):
    x = x_ref[...]
    y3, g13 = _rows_mod(_dot(m_ref[...], w_ref[...]), g1_ref[...])
    x1 = x + (y3 * g13).reshape(x.shape)
    x1_ref[...] = x1
    h2 = _modulated_norm(x1, gn_ref[...], sc_ref[...], sh_ref[...])
    hi = h2.astype(BF16)
    h2_ref[...] = hi
    lo = (h2 - hi.astype(F32)).astype(BF16)
    wr_hi = wr_hi_ref[...]
    logits = _dot(hi, wr_hi) + (_dot(lo, wr_hi) + _dot(hi, wr_lo_ref[...]))
    gates_ref[...] = _route(logits + br_ref[...])


def _out_proj(merged, x, w_out_b, layer, mod3, n_prompt_tiles, norm_g, w_router, b_router):
    n, d = x.shape
    w_router_hi = w_router.astype(BF16)
    row = pl.BlockSpec((ROW_TILE, d), lambda i: (i, 0))
    small = pl.BlockSpec((ROW_TILE, ROUTER_LANES), lambda i: (i, 0))
    return pl.pallas_call(
        _out_kernel, grid=(n // ROW_TILE,),
        in_specs=[row, row,
                  pl.BlockSpec((None, d, d), lambda i: (layer, 0, 0)),
                  _mod_spec(n_prompt_tiles, d, 2),
                  pl.BlockSpec((1, d), lambda i: (0, 0)),
                  _mod_spec(n_prompt_tiles, d, 4),
                  _mod_spec(n_prompt_tiles, d, 3),
                  pl.BlockSpec((d, ROUTER_LANES), lambda i: (0, 0)),
                  pl.BlockSpec((d, ROUTER_LANES), lambda i: (0, 0)),
                  pl.BlockSpec((1, ROUTER_LANES), lambda i: (0, 0))],
        out_specs=[row, row, small],
        out_shape=[jax.ShapeDtypeStruct((n, d), F32), jax.ShapeDtypeStruct((n, d), BF16),
                   jax.ShapeDtypeStruct((n, ROUTER_LANES), F32)],
        compiler_params=_params("arbitrary"), name="out_proj",
    )(merged, x, w_out_b, mod3, norm_g.reshape(1, d), mod3, mod3, w_router_hi,
      (w_router - w_router_hi.astype(F32)).astype(BF16), b_router)


def _moe_kernel(h_ref, gates_ref, wg_ref, wu_ref, wd_ref, o_ref):
    e = pl.program_id(1)
    h = h_ref[...]
    gates = gates_ref[...]
    lane = lax.broadcasted_iota(jnp.int32, gates.shape, 1)
    gate = jnp.sum(jnp.where(lane == e, gates, 0.0), axis=-1, keepdims=True)
    hg = _dot(h, wg_ref[...])
    hu = _dot(h, wu_ref[...])
    act = (hg * jax.nn.sigmoid(hg)) * hu * gate
    contrib = _dot(act.astype(BF16), wd_ref[...])

    @pl.when(e == 0)
    def _():
        o_ref[...] = contrib

    @pl.when(e > 0)
    def _():
        o_ref[...] += contrib


def _moe(h2, gates, wg_b, wu_b, wd_b, layer):
    n, d = h2.shape
    f = wg_b.shape[3]
    n_exp = wg_b.shape[1]
    assert n % MOE_ROW_TILES == 0
    tm = n // MOE_ROW_TILES
    assert tm % 16 == 0
    row = pl.BlockSpec((tm, d), lambda i, e: (i, 0))
    return pl.pallas_call(
        _moe_kernel, grid=(MOE_ROW_TILES, n_exp),
        in_specs=[row,
                  pl.BlockSpec((tm, ROUTER_LANES), lambda i, e: (i, 0)),
                  pl.BlockSpec((None, None, d, f), lambda i, e: (layer, e, 0, 0)),
                  pl.BlockSpec((None, None, d, f), lambda i, e: (layer, e, 0, 0)),
                  pl.BlockSpec((None, None, f, d), lambda i, e: (layer, e, 0, 0))],
        out_specs=row,
        out_shape=jax.ShapeDtypeStruct((n, d), F32),
        compiler_params=_params("arbitrary", "arbitrary"), name="moe",
    )(h2, gates, wg_b, wu_b, wd_b)


def _residual_kernel(x_ref, f_ref, g2_ref, op_ref, os_ref, *, n_prompt_tiles):
    f3, g23 = _rows_mod(f_ref[...], g2_ref[...])
    x = x_ref[...] + (f3 * g23).reshape(x_ref.shape)

    def store(ref):
        ref[...] = x

    _store_split(n_prompt_tiles, op_ref, os_ref, store)


def _residual(x1, ffn, mod3, n_prompt_tiles):
    n, d = x1.shape
    seq = n_prompt_tiles * ROW_TILE
    row = pl.BlockSpec((ROW_TILE, d), lambda i: (i, 0))
    return pl.pallas_call(
        functools.partial(_residual_kernel, n_prompt_tiles=n_prompt_tiles), grid=(n // ROW_TILE,),
        in_specs=[row, row, _mod_spec(n_prompt_tiles, d, 5)],
        out_specs=_split_specs(n_prompt_tiles, (ROW_TILE, d)),
        out_shape=[jax.ShapeDtypeStruct((seq, d), F32), jax.ShapeDtypeStruct((n - seq, d), F32)],
        compiler_params=_params("arbitrary"), name="residual",
    )(x1, ffn, mod3)


def kernel(x_prompt, x_sample, c_prompt, c_sample, cache_k, cache_v, state_ssm_re, state_ssm_im, page_table, w_ada, b_ada, norm_mix_g, w_in, ssm_a_re, ssm_a_im, ssm_log_dt, ssm_b_re, ssm_b_im, ssm_c_re, ssm_c_im, ssm_d, w_glu, b_glu, q_norm_g, k_norm_g, sb_bias, w_branch_ssm, w_branch_attn, w_out, norm_ffn_g, w_router_group, b_router_group, w_router_expert, b_router_expert, w_exp_gate, w_exp_up, w_exp_down):
    n_layers = w_ada.shape[0]
    bsz, seq, d = x_prompt.shape
    n_seq, n_q, _ = x_sample.shape
    assert bsz == 1 and n_q == SUBLANES and n_seq * n_q == ROW_TILE and seq % ROW_TILE == 0
    n_s = n_seq * n_q
    n_prompt_tiles = seq // ROW_TILE
    mod_rows = ROW_TILE // SUBLANES
    ssm_w = ssm_d.shape[1]
    attn_w = w_branch_attn.shape[1]
    assert ssm_w == attn_w
    n_heads = attn_w // HEAD_DIM
    n_groups = ssm_w // SSM_GROUP
    page = cache_k.shape[2]
    chunk_p, chunk_s = 16, n_q
    assert seq % chunk_p == 0 and chunk_p == 2 * chunk_s and page == HEAD_DIM

    w_in_b = w_in.astype(BF16)
    w_glu_b = w_glu.astype(BF16)
    w_bs_b = w_branch_ssm.astype(BF16)
    w_ba_b = w_branch_attn.astype(BF16)
    w_out_b = w_out.astype(BF16)
    wg_b, wu_b, wd_b = w_exp_gate.astype(BF16), w_exp_up.astype(BF16), w_exp_down.astype(BF16)

    c_all = jnp.concatenate([c_prompt, c_sample], axis=0)
    pad = -c_all.shape[0] % SUBLANES
    mod = _adaln(jnp.pad(c_all, ((0, pad), (0, 0))), w_ada, b_ada)

    x = jnp.concatenate([x_prompt.reshape(seq, d), x_sample.reshape(n_s, d)], axis=0)
    outs = {k: [] for k in ("rp", "ip", "rs", "is")}
    kv_out = None
    for l in range(n_layers):
        mod3_prev = mod3 if l else None
        mod3 = jnp.stack([jnp.broadcast_to(mod[l, 0], (mod_rows, N_MOD * d)), mod[l, 1:1 + n_seq]])
        if l == 0:
            h = _norm_mod(x, norm_mix_g[l], mod3, n_prompt_tiles, 1, 0)
        else:
            x, h = _res_norm_mod(x1, ffn, mod3_prev, norm_mix_g[l], mod3, n_prompt_tiles)
        u, u_b, q_b, k_b, v_b, gates, kv_out = _in_proj(h, w_in_b, l, q_norm_g[l], k_norm_g[l], ssm_w,
                                                        n_prompt_tiles, kv_out)

        ssm_par = (ssm_a_re[l], ssm_a_im[l], ssm_log_dt[l], ssm_b_re[l], ssm_b_im[l], ssm_c_re[l], ssm_c_im[l])
        prep_p = _ssm_prep(*ssm_par, chunk_p)
        up = _to_pairs(u_b[:seq], seq // chunk_p, chunk_p)
        loc_re, loc_im = _ssm_loc(up, prep_p[1], prep_p[2])
        sp_re, sp_im = _ssm_scan(loc_re, loc_im, prep_p[5], prep_p[6])
        yp, np_re, np_im = _ssm_out(up, sp_re, sp_im, prep_p, False)
        us = _to_pairs(u_b[seq:], n_seq, chunk_s)
        h0_re = state_ssm_re[l].reshape(n_seq, n_groups * SSM_STATE)
        h0_im = state_ssm_im[l].reshape(n_seq, n_groups * SSM_STATE)
        ys, ns_re, ns_im = _ssm_out(us, h0_re, h0_im, prep_p, True)
        y = jnp.concatenate([_from_pairs(yp, seq // chunk_p, chunk_p), _from_pairs(ys, n_seq, chunk_s)], axis=0)
        a_branch = _glu(y, u, ssm_d[l], w_glu_b, l, b_glu[l])

        bias2 = sb_bias[l].astype(F32) * LOG2_E
        bp = _attn_prompt(q_b, k_b, v_b, bias2, seq)
        new_pad = ((0, 0), (0, page - n_q), (0, 0))
        bs = _attn_sample(q_b[seq:].reshape(n_seq, n_q, attn_w),
                          jnp.pad(k_b[seq:].reshape(n_seq, n_q, attn_w), new_pad),
                          jnp.pad(v_b[seq:].reshape(n_seq, n_q, attn_w), new_pad),
                          cache_k, cache_v, l, page_table, bias2)
        b_branch = jnp.concatenate([bp, bs.reshape(n_s, attn_w).astype(BF16)], axis=0)

        merged = _merge(a_branch, b_branch, gates, w_bs_b, w_ba_b, l)
        w_router = jnp.pad(jnp.concatenate([w_router_expert[l], w_router_group[l]], axis=1),
                           ((0, 0), (0, ROUTER_LANES - N_EXPERTS - MOE_GROUPS)))
        b_router = jnp.pad(jnp.concatenate([b_router_expert[l], b_router_group[l]]),
                           (0, ROUTER_LANES - N_EXPERTS - MOE_GROUPS)).reshape(1, ROUTER_LANES)
        x1, h2, route = _out_proj(merged, x, w_out_b, l, mod3, n_prompt_tiles, norm_ffn_g[l], w_router, b_router)
        ffn = _moe(h2, route, wg_b, wu_b, wd_b, l)

        outs["rp"].append(np_re[-1].reshape(1, n_groups, SSM_STATE))
        outs["ip"].append(np_im[-1].reshape(1, n_groups, SSM_STATE))
        outs["rs"].append(ns_re.reshape(n_seq, n_groups, SSM_STATE))
        outs["is"].append(ns_im.reshape(n_seq, n_groups, SSM_STATE))

    y_p, y_s = _residual(x1, ffn, mod3, n_prompt_tiles)
    st = lambda k: jnp.stack(outs[k])
    kp, ks, vp, vs = kv_out
    return (y_p.reshape(1, seq, d), y_s.reshape(n_seq, n_q, d),
            kp.reshape(n_layers, 1, seq, n_heads, HEAD_DIM), vp.reshape(n_layers, 1, seq, n_heads, HEAD_DIM),
            st("rp"), st("ip"),
            ks.reshape(n_layers, n_seq, n_q, n_heads, HEAD_DIM), vs.reshape(n_layers, n_seq, n_q, n_heads, HEAD_DIM),
            st("rs"), st("is"))
```

```python
import functools
import math

import jax
import jax.numpy as jnp
from jax import lax
from jax.experimental import pallas as pl
from jax.experimental.pallas import tpu as pltpu

F32 = jnp.float32
BF16 = jnp.bfloat16

RMS_EPS = 1e-6
HEAD_DIM = 128
SSM_GROUP = 16
SSM_STATE = 64
PAIR_CH = 2 * SSM_GROUP
PAIR_ST = 2 * SSM_STATE
LANES = 128
PAIRS_PER_SLAB = LANES // PAIR_CH
MOE_GROUPS = 4
EXPERTS_PER_GROUP = 4
N_EXPERTS = MOE_GROUPS * EXPERTS_PER_GROUP
ROUTER_LANES = 128
N_MOD = 6
ROW_TILE = 256
SUBLANES = 8
VMEM_LIMIT_BYTES = 56 * 1024 * 1024
ATTN_TQ = 256
ATTN_TK = 256
MOE_ROW_TILES = 8
PROMPT_TILES_PER_BODY = 8
LOG2_E = 1.4426950408889634
Q_SCALE_LOG2 = HEAD_DIM ** -0.5 * LOG2_E


def _params(*sem):
    return pltpu.CompilerParams(dimension_semantics=sem, vmem_limit_bytes=VMEM_LIMIT_BYTES)


def _dot(a, b):
    return jnp.dot(a, b, preferred_element_type=F32)


def _dot_nt(a, b):
    return lax.dot_general(a, b, (((1,), (1,)), ((), ())), preferred_element_type=F32)


def _dot_exact(a, b):
    return jnp.dot(a, b, preferred_element_type=F32, precision=lax.Precision.HIGHEST)


def _adaln_kernel(c_ref, w_ref, b_ref, o_ref):
    c = c_ref[...]
    a = (c * jax.nn.sigmoid(c)).astype(BF16)
    o_ref[...] = _dot(a, w_ref[...].astype(BF16)) + b_ref[...]


def _adaln(c_all, w_ada, b_ada):
    n_layers, d, n_out = w_ada.shape
    r = c_all.shape[0]
    tn = 1024
    return pl.pallas_call(
        _adaln_kernel,
        grid=(n_layers, n_out // tn),
        in_specs=[pl.BlockSpec((r, d), lambda l, j: (0, 0)),
                  pl.BlockSpec((None, d, tn), lambda l, j: (l, 0, j)),
                  pl.BlockSpec((None, 1, tn), lambda l, j: (l, 0, j))],
        out_specs=pl.BlockSpec((None, r, tn), lambda l, j: (l, 0, j)),
        out_shape=jax.ShapeDtypeStruct((n_layers, r, n_out), F32),
        compiler_params=_params("arbitrary", "arbitrary"),
        name="adaln",
    )(c_all, w_ada, b_ada.reshape(n_layers, 1, n_out))


def _rows_mod(v, m):
    tm, d = v.shape
    return v.reshape(tm // SUBLANES, SUBLANES, d), m[:, None, :]


def _modulated_norm(x, g, sc, sh):
    ms = jnp.mean(x * x, axis=-1, keepdims=True)
    y = x * lax.rsqrt(ms + RMS_EPS) * g
    y3, sc3 = _rows_mod(y, sc)
    h = y3 * (1.0 + sc3) + sh[:, None, :]
    return h.reshape(x.shape)


def _mod_spec(n_prompt_tiles, d, which):
    return pl.BlockSpec((None, ROW_TILE // SUBLANES, d),
                        lambda i: (jnp.where(i < n_prompt_tiles, 0, 1), 0, which))


def _norm_mod_kernel(x_ref, g_ref, sc_ref, sh_ref, o_ref):
    o_ref[...] = _modulated_norm(x_ref[...], g_ref[...], sc_ref[...], sh_ref[...]).astype(BF16)


def _res_norm_mod_kernel(x1_ref, f_ref, g2_ref, g_ref, sc_ref, sh_ref, x_ref, h_ref):
    f3, g23 = _rows_mod(f_ref[...], g2_ref[...])
    x = x1_ref[...] + (f3 * g23).reshape(x_ref.shape)
    x_ref[...] = x
    h_ref[...] = _modulated_norm(x, g_ref[...], sc_ref[...], sh_ref[...]).astype(BF16)


def _res_norm_mod(x1, ffn, mod3_prev, g, mod3, n_prompt_tiles):
    n, d = x1.shape
    row = pl.BlockSpec((ROW_TILE, d), lambda i: (i, 0))
    return pl.pallas_call(
        _res_norm_mod_kernel,
        grid=(n // ROW_TILE,),
        in_specs=[row, row, _mod_spec(n_prompt_tiles, d, 5),
                  pl.BlockSpec((1, d), lambda i: (0, 0)),
                  _mod_spec(n_prompt_tiles, d, 1), _mod_spec(n_prompt_tiles, d, 0)],
        out_specs=[row, row],
        out_shape=[jax.ShapeDtypeStruct((n, d), F32), jax.ShapeDtypeStruct((n, d), BF16)],
        compiler_params=_params("arbitrary"),
        name="res_norm_mod",
    )(x1, ffn, mod3_prev, g.reshape(1, d), mod3, mod3)


def _norm_mod(x, g, mod3, n_prompt_tiles, sc_idx, sh_idx):
    n, d = x.shape
    return pl.pallas_call(
        _norm_mod_kernel,
        grid=(n // ROW_TILE,),
        in_specs=[pl.BlockSpec((ROW_TILE, d), lambda i: (i, 0)),
                  pl.BlockSpec((1, d), lambda i: (0, 0)),
                  _mod_spec(n_prompt_tiles, d, sc_idx),
                  _mod_spec(n_prompt_tiles, d, sh_idx)],
        out_specs=pl.BlockSpec((ROW_TILE, d), lambda i: (i, 0)),
        out_shape=jax.ShapeDtypeStruct((n, d), BF16),
        compiler_params=_params("arbitrary"),
        name="norm_mod",
    )(x, g.reshape(1, d), mod3, mod3)


def _store_token_major(ref, val):
    tm = val.shape[0]
    n_heads = val.shape[1] // HEAD_DIM
    for hh in range(n_heads):
        ref[pl.ds(hh, tm, stride=n_heads), :] = val[:, hh * HEAD_DIM:(hh + 1) * HEAD_DIM]


def _split_specs(n_prompt_tiles, block, lead=()):
    zeros = (0,) * (len(block) - 1)
    nones = (None,) * len(lead)
    return [pl.BlockSpec(nones + block, lambda i: lead + (jnp.minimum(i, n_prompt_tiles - 1),) + zeros),
            pl.BlockSpec(nones + block, lambda i: lead + (0,) + zeros)]


def _store_split(n_prompt_tiles, prompt_ref, sample_ref, store):
    i = pl.program_id(0)

    @pl.when(i < n_prompt_tiles)
    def _():
        store(prompt_ref)

    @pl.when(i >= n_prompt_tiles)
    def _():
        store(sample_ref)


def _proj_uv_kernel(h_ref, wu_ref, wv_ref, *rest, n_prompt_tiles, chunks):
    u_ref, up_ref, us_ref, vb_ref, vp_ref, vs_ref, slab_ref = rest[-7:]
    h = h_ref[...]
    u = _dot(h, wu_ref[...])
    u_ref[...] = u
    _store_split(n_prompt_tiles, up_ref, us_ref,
                 lambda ref: _rows_to_pairs(slab_ref, u, ref, chunks[0] if ref is up_ref else chunks[1]))
    v = _dot(h, wv_ref[...])
    vb_ref[...] = v.astype(BF16)
    _store_split(n_prompt_tiles, vp_ref, vs_ref, lambda ref: _store_token_major(ref, v))


def _head_norm(acc, g):
    outs = []
    for hh in range(acc.shape[1] // HEAD_DIM):
        blk = acc[:, hh * HEAD_DIM:(hh + 1) * HEAD_DIM]
        ms = jnp.mean(blk * blk, axis=-1, keepdims=True)
        outs.append(blk * lax.rsqrt(ms + RMS_EPS) * g)
    return jnp.concatenate(outs, axis=1)


def _proj_qk_kernel(h_ref, wq_ref, wk_ref, gq_ref, gk_ref, *rest, n_prompt_tiles):
    qb_ref, kb_ref, kp_ref, ks_ref = rest[-4:]
    h = h_ref[...]
    qb_ref[...] = (_head_norm(_dot(h, wq_ref[...]), gq_ref[...]) * Q_SCALE_LOG2).astype(BF16)
    k = _head_norm(_dot(h, wk_ref[...]), gk_ref[...])
    kb_ref[...] = k.astype(BF16)
    _store_split(n_prompt_tiles, kp_ref, ks_ref, lambda ref: _store_token_major(ref, k))


def _proj_gate_kernel(h_ref, w_ref, o_ref):
    o_ref[...] = jax.nn.sigmoid(_dot(h_ref[...], w_ref[...]))


def _in_proj(h, w_in_b, layer, q_g, k_g, width, n_prompt_tiles, kv_prev, chunks):
    n, d = h.shape
    nt = n // ROW_TILE
    n_layers = w_in_b.shape[0]
    row = pl.BlockSpec((ROW_TILE, d), lambda i: (i, 0))
    out = pl.BlockSpec((ROW_TILE, width), lambda i: (i, 0))

    def wcol(j):
        return pl.BlockSpec((None, d, width), lambda i: (layer, 0, j))

    sds = lambda dt: jax.ShapeDtypeStruct((n, width), dt)
    n_heads = width // HEAD_DIM
    tok_rows = ROW_TILE * n_heads
    tok = _split_specs(n_prompt_tiles, (tok_rows, HEAD_DIM), (layer,))
    tok_sds = [jax.ShapeDtypeStruct((n_layers, n_prompt_tiles * tok_rows, HEAD_DIM), F32),
               jax.ShapeDtypeStruct((n_layers, (nt - n_prompt_tiles) * tok_rows, HEAD_DIM), F32)]
    assert nt - n_prompt_tiles == 1
    if kv_prev is None:
        prev_k, prev_v, prev_specs = (), (), []
    else:
        prev_k, prev_v = kv_prev[:2], kv_prev[2:]
        prev_specs = [pl.BlockSpec(memory_space=pl.ANY)] * 2
    n_prev = len(prev_specs)
    n_pairs = width // PAIR_CH
    pair_sds = [jax.ShapeDtypeStruct((n_pairs, n_prompt_tiles * ROW_TILE // chunks[0], chunks[0] * PAIR_CH), BF16),
                jax.ShapeDtypeStruct((n_pairs, ROW_TILE // chunks[1], chunks[1] * PAIR_CH), BF16)]
    u, u_p, u_s, v_b, v_p, v_s = pl.pallas_call(
        functools.partial(_proj_uv_kernel, n_prompt_tiles=n_prompt_tiles, chunks=chunks), grid=(nt,),
        in_specs=[row, wcol(0), wcol(3)] + prev_specs,
        out_specs=[out] + _pair_specs(n_prompt_tiles, n_pairs, chunks) + [out] + tok,
        out_shape=[sds(F32)] + pair_sds + [sds(BF16)] + tok_sds,
        input_output_aliases={3 + j: 4 + j for j in range(n_prev)},
        scratch_shapes=[pltpu.VMEM((width // LANES, ROW_TILE, LANES), F32)],
        compiler_params=_params("arbitrary"), name="proj_uv",
    )(h, w_in_b, w_in_b, *prev_v)
    gspec = pl.BlockSpec((1, HEAD_DIM), lambda i: (0, 0))
    q_b, k_b, k_p, k_s = pl.pallas_call(
        functools.partial(_proj_qk_kernel, n_prompt_tiles=n_prompt_tiles), grid=(nt,),
        in_specs=[row, wcol(1), wcol(2), gspec, gspec] + prev_specs, out_specs=[out, out] + tok,
        out_shape=[sds(BF16), sds(BF16)] + tok_sds,
        input_output_aliases={5 + j: 2 + j for j in range(n_prev)},
        compiler_params=_params("arbitrary"), name="proj_qk",
    )(h, w_in_b, w_in_b, q_g.reshape(1, HEAD_DIM), k_g.reshape(1, HEAD_DIM), *prev_k)
    n_gate = (w_in_b.shape[2] - 4 * width) // width
    gates = pl.pallas_call(
        _proj_gate_kernel, grid=(n_gate, nt),
        in_specs=[pl.BlockSpec((ROW_TILE, d), lambda j, i: (i, 0)),
                  pl.BlockSpec((None, d, width), lambda j, i: (layer, 0, 4 + j))],
        out_specs=pl.BlockSpec((ROW_TILE, width), lambda j, i: (i, j)),
        out_shape=jax.ShapeDtypeStruct((n, n_gate * width), F32),
        compiler_params=_params("arbitrary", "arbitrary"), name="proj_gate",
    )(h, w_in_b)
    return u, u_p, u_s, q_b, k_b, v_b, gates, (k_p, k_s, v_p, v_s)


def _cmul(ar, ai, br, bi):
    return ar * br - ai * bi, ar * bi + ai * br


def _discretise(a_re, a_im, log_dt):
    dt = jnp.exp(log_dt)
    mag = jnp.exp(dt * a_re)
    ab_re = mag * jnp.cos(dt * a_im)
    ab_im = mag * jnp.sin(dt * a_im)
    den = a_re * a_re + a_im * a_im
    f_re = ((ab_re - 1.0) * a_re + ab_im * a_im) / den
    f_im = (ab_im * a_re - (ab_re - 1.0) * a_im) / den
    return dt, f_re, f_im


def _abar_pow(a_re, a_im, dt, k):
    mag = jnp.exp(k * dt * a_re)
    th = k * dt * a_im
    return mag * jnp.cos(th), mag * jnp.sin(th)


def _ssm_prep_kernel(as_re_ref, as_im_ref, ldts_ref, al_re_ref, al_im_ref, ldtl_ref,
                     bt_re_ref, bt_im_ref, btr_re_ref, btr_im_ref, c_re_ref, c_im_ref,
                     ct_re_ref, ct_im_ref,
                     m_ref, e_re_ref, e_im_ref, f_re_ref, f_im_ref, p_re_ref, p_im_ref, *, chunk):
    width = chunk * PAIR_CH
    as_re, as_im = as_re_ref[...], as_im_ref[...]
    dts, fs_re, fs_im = _discretise(as_re, as_im, ldts_ref[...])
    al_re, al_im = al_re_ref[...], al_im_ref[...]
    dtl, fl_re, fl_im = _discretise(al_re, al_im, ldtl_ref[...])

    lane_blk = lax.shift_right_logical(lax.broadcasted_iota(jnp.int32, (1, width), 1), 5)
    assert PAIR_CH == 32

    k_lane = lax.broadcasted_iota(jnp.int32, (1, LANES), 1)
    assert chunk < LANES
    pw_re, pw_im = _abar_pow(as_re, as_im, dts, jnp.minimum(k_lane, chunk).astype(F32))
    k_row = lax.broadcasted_iota(jnp.int32, (LANES, width), 0)

    def spread(lane_to_k):
        onehot = (k_row == lane_to_k).astype(F32)
        return _dot_exact(pw_re, onehot), _dot_exact(pw_im, onehot)

    pk_re, pk_im = spread(chunk - 1 - lane_blk)
    bb_re, bb_im = _cmul(fs_re, fs_im, bt_re_ref[...], bt_im_ref[...])
    w_re, w_im = _cmul(pk_re, pk_im, bb_re, bb_im)
    kr = _dot_exact(c_re_ref[...], w_re) - _dot_exact(c_im_ref[...], w_im)
    lane = lax.broadcasted_iota(jnp.int32, (PAIR_CH, width), 1)
    for t in range(chunk):
        shift = (chunk - 1 - t) * PAIR_CH
        rolled = kr if shift == 0 else pltpu.roll(kr, width - shift, axis=1)
        m_ref[t * PAIR_CH:(t + 1) * PAIR_CH, :] = jnp.where(lane < (t + 1) * PAIR_CH, rolled, 0.0)

    bbt_re, bbt_im = _cmul(fl_re, fl_im, btr_re_ref[...], btr_im_ref[...])
    for tau in range(chunk):
        q_re, q_im = _abar_pow(al_re, al_im, dtl, float(chunk - 1 - tau))
        e_re, e_im = _cmul(q_re, q_im, bbt_re, bbt_im)
        e_re_ref[tau * PAIR_CH:(tau + 1) * PAIR_CH, :] = e_re
        e_im_ref[tau * PAIR_CH:(tau + 1) * PAIR_CH, :] = e_im

    pf_re, pf_im = spread(lane_blk + 1)
    ca_re, ca_im = _cmul(ct_re_ref[...], ct_im_ref[...], pf_re, pf_im)
    f_re_ref[...] = ca_re
    f_im_ref[...] = -ca_im

    pt_re, pt_im = _abar_pow(al_re, al_im, dtl, float(chunk))
    ph_re, ph_im = _abar_pow(al_re, al_im, dtl, float(chunk // 2))
    p_re_ref[...] = jnp.concatenate([pt_re, ph_re], axis=0)
    p_im_ref[...] = jnp.concatenate([pt_im, ph_im], axis=0)


def _ssm_layouts(a_re, a_im, log_dt, b_re, b_im, c_re, c_im, chunk):
    n_groups = a_re.shape[0]
    n_pairs = n_groups // 2
    eye = jnp.eye(2, dtype=F32)

    def sub(x):
        return x.reshape(n_pairs, PAIR_ST, 1)

    def lan(x):
        return x.reshape(n_pairs, 1, PAIR_ST)

    ldt = jnp.broadcast_to(log_dt[:, None], (n_groups, SSM_STATE))

    def b_tiled(b):
        bp = b.reshape(n_pairs, 2, SSM_STATE, SSM_GROUP)
        bd = jnp.einsum('ngph,gk->ngpkh', bp, eye).reshape(n_pairs, PAIR_ST, PAIR_CH)
        return jnp.tile(bd, (1, 1, chunk))

    def b_rows(b):
        bp = b.reshape(n_pairs, 2, SSM_STATE, SSM_GROUP)
        return jnp.einsum('ngph,gk->nghkp', bp, eye).reshape(n_pairs, PAIR_CH, PAIR_ST)

    def c_rows(c):
        cp = c.reshape(n_pairs, 2, SSM_GROUP, SSM_STATE)
        return jnp.einsum('nghp,gk->nghkp', cp, eye).reshape(n_pairs, PAIR_CH, PAIR_ST)

    def c_tiled(c):
        cp = c.reshape(n_pairs, 2, SSM_GROUP, SSM_STATE)
        cd = jnp.einsum('nghp,gk->nkpgh', cp, eye).reshape(n_pairs, PAIR_ST, PAIR_CH)
        return jnp.tile(cd, (1, 1, chunk))

    return (sub(a_re), sub(a_im), sub(ldt), lan(a_re), lan(a_im), lan(ldt),
            b_tiled(b_re), b_tiled(b_im), b_rows(b_re), b_rows(b_im),
            c_rows(c_re), c_rows(c_im), c_tiled(c_re), c_tiled(c_im))


def _ssm_prep(a_re, a_im, log_dt, b_re, b_im, c_re, c_im, chunk):
    n_pairs = a_re.shape[0] // 2
    width = chunk * PAIR_CH
    args = _ssm_layouts(a_re, a_im, log_dt, b_re, b_im, c_re, c_im, chunk)

    def spec(r, c):
        return pl.BlockSpec((None, r, c), lambda i: (i, 0, 0))

    sub, lan = spec(PAIR_ST, 1), spec(1, PAIR_ST)
    wide, rows = spec(PAIR_ST, width), spec(PAIR_CH, PAIR_ST)
    sds = lambda r, c: jax.ShapeDtypeStruct((n_pairs, r, c), F32)
    return pl.pallas_call(
        functools.partial(_ssm_prep_kernel, chunk=chunk),
        grid=(n_pairs,),
        in_specs=[sub, sub, sub, lan, lan, lan, wide, wide, rows, rows, rows, rows, wide, wide],
        out_specs=[spec(width, width), spec(width, PAIR_ST), spec(width, PAIR_ST),
                   wide, wide, spec(2, PAIR_ST), spec(2, PAIR_ST)],
        out_shape=[sds(width, width), sds(width, PAIR_ST), sds(width, PAIR_ST),
                   sds(PAIR_ST, width), sds(PAIR_ST, width), sds(2, PAIR_ST), sds(2, PAIR_ST)],
        compiler_params=_params("arbitrary"),
        name="ssm_prep",
    )(*args)


def _ssm_loc_kernel(u_ref, e_re_ref, e_im_ref, l_re_ref, l_im_ref):
    ub = u_ref[...]
    l_re_ref[...] = _dot(ub, e_re_ref[...].astype(BF16))
    l_im_ref[...] = _dot(ub, e_im_ref[...].astype(BF16))


def _ssm_loc(u_pairs, e_re, e_im):
    n_pairs, rows, width = u_pairs.shape
    st = pl.BlockSpec((rows, PAIR_ST), lambda i: (0, i))
    sds = jax.ShapeDtypeStruct((rows, n_pairs * PAIR_ST), F32)
    espec = pl.BlockSpec((None, width, PAIR_ST), lambda i: (i, 0, 0))
    return pl.pallas_call(
        _ssm_loc_kernel, grid=(n_pairs,),
        in_specs=[pl.BlockSpec((None, rows, width), lambda i: (i, 0, 0)), espec, espec],
        out_specs=[st, st], out_shape=[sds, sds],
        compiler_params=_params("arbitrary"), name="ssm_loc",
    )(u_pairs, e_re, e_im)


def _ssm_scan_kernel(l_re_ref, l_im_ref, p_re_ref, p_im_ref, s_re_ref, s_im_ref):
    p_re, p_im = p_re_ref[...], p_im_ref[...]
    n_chunks = l_re_ref.shape[0]

    def body(c, s):
        s_re, s_im = s
        s_re_ref[pl.ds(c, 1), :] = s_re
        s_im_ref[pl.ds(c, 1), :] = s_im
        n_re, n_im = _cmul(p_re, p_im, s_re, s_im)
        return n_re + l_re_ref[pl.ds(c, 1), :], n_im + l_im_ref[pl.ds(c, 1), :]

    zero = jnp.zeros_like(p_re)
    lax.fori_loop(0, n_chunks, body, (zero, zero))


def _ssm_scan(loc_re, loc_im, p_re, p_im):
    rows, n_state = loc_re.shape
    tc = 1024
    blk = pl.BlockSpec((rows, tc), lambda j: (0, j))
    par = pl.BlockSpec((1, tc), lambda j: (0, j))
    sds = jax.ShapeDtypeStruct((rows, n_state), F32)
    return pl.pallas_call(
        _ssm_scan_kernel, grid=(n_state // tc,),
        in_specs=[blk, blk, par, par], out_specs=[blk, blk], out_shape=[sds, sds],
        compiler_params=_params("arbitrary"), name="ssm_scan",
    )(loc_re, loc_im, p_re[:, 0].reshape(1, n_state), p_im[:, 0].reshape(1, n_state))


def _gelu_tanh(x):
    c = math.sqrt(2.0 / math.pi)
    return 0.5 * x * (1.0 + jnp.tanh(c * (x + 0.044715 * (x * x * x))))


def _ssm_out_kernel(u_ref, s_re_ref, s_im_ref, m_ref, e_re_ref, e_im_ref, f_re_ref, f_im_ref,
                    p_re_ref, p_im_ref, y_ref, n_re_ref, n_im_ref):
    ub = u_ref[...]
    s_re, s_im = s_re_ref[...], s_im_ref[...]
    y = (_dot_nt(ub, m_ref[...].astype(BF16))
         + _dot(s_re.astype(BF16), f_re_ref[...].astype(BF16))
         + _dot(s_im.astype(BF16), f_im_ref[...].astype(BF16)))
    y_ref[...] = y.astype(BF16)
    a_re, a_im = _cmul(p_re_ref[...], p_im_ref[...], s_re, s_im)
    n_re_ref[...] = a_re + _dot(ub, e_re_ref[...].astype(BF16))
    n_im_ref[...] = a_im + _dot(ub, e_im_ref[...].astype(BF16))


def _ssm_out(u_pairs, s_re, s_im, prep, half):
    n_pairs, rows, width = u_pairs.shape
    m, e_re, e_im, f_re, f_im, p_re, p_im = prep
    st = pl.BlockSpec((rows, PAIR_ST), lambda i: (0, i))
    per = lambda r, c: pl.BlockSpec((None, r, c), lambda i: (i, 0, 0))
    e_spec = pl.BlockSpec((None, width, PAIR_ST), lambda i: (i, 1 if half else 0, 0))
    p_spec = pl.BlockSpec((None, 1, PAIR_ST), lambda i: (i, 0, 0))
    sds = jax.ShapeDtypeStruct((rows, n_pairs * PAIR_ST), F32)
    pick = slice(1, 2) if half else slice(0, 1)
    return pl.pallas_call(
        _ssm_out_kernel, grid=(n_pairs,),
        in_specs=[per(rows, width), st, st, per(width, width), e_spec, e_spec,
                  per(PAIR_ST, width), per(PAIR_ST, width), p_spec, p_spec],
        out_specs=[per(rows, width), st, st],
        out_shape=[jax.ShapeDtypeStruct((n_pairs, rows, width), BF16), sds, sds],
        compiler_params=_params("arbitrary"), name="ssm_out",
    )(u_pairs, s_re, s_im, m, e_re, e_im, f_re, f_im, p_re[:, pick], p_im[:, pick])


def _rows_to_pairs(slab_ref, val, out_ref, chunk):
    tm, c = val.shape
    rows = tm // chunk
    for s in range(c // LANES):
        slab_ref[s] = val[:, s * LANES:(s + 1) * LANES]
    for step in range(chunk):
        dst = (step % PAIRS_PER_SLAB) * PAIR_CH
        tile0 = (step // PAIRS_PER_SLAB) * LANES
        for s in range(c // LANES):
            x = slab_ref[s, pl.ds(step, rows, stride=chunk), :]
            for j in range(PAIRS_PER_SLAB):
                shift = (dst - j * PAIR_CH) % LANES
                moved = x if shift == 0 else pltpu.roll(x, shift, axis=1)
                out_ref[s * PAIRS_PER_SLAB + j, :, tile0 + dst:tile0 + dst + PAIR_CH] = (
                    moved[:, dst:dst + PAIR_CH].astype(BF16))


def _pairs_to_rows(in_ref, slab_ref, chunk):
    n_pairs, rows, _ = in_ref.shape
    lane_pair = lax.shift_right_logical(lax.broadcasted_iota(jnp.int32, (rows, LANES), 1), 5)
    assert PAIR_CH == 32
    for step in range(chunk):
        src = (step % PAIRS_PER_SLAB) * PAIR_CH
        tile0 = (step // PAIRS_PER_SLAB) * LANES
        for s in range(n_pairs // PAIRS_PER_SLAB):
            acc = None
            for j in range(PAIRS_PER_SLAB):
                x = in_ref[s * PAIRS_PER_SLAB + j, :, tile0:tile0 + LANES].astype(F32)
                shift = (j * PAIR_CH - src) % LANES
                moved = x if shift == 0 else pltpu.roll(x, shift, axis=1)
                acc = moved if acc is None else jnp.where(lane_pair == j, moved, acc)
            slab_ref[s, pl.ds(step, rows, stride=chunk), :] = acc


def _glu_kernel(yp_ref, ys_ref, u_ref, d_ref, w_ref, b_ref, o_ref, slab_ref, *, n_prompt_tiles, chunks):
    i = pl.program_id(0)

    @pl.when(i < n_prompt_tiles)
    def _():
        _pairs_to_rows(yp_ref, slab_ref, chunks[0])

    @pl.when(i >= n_prompt_tiles)
    def _():
        _pairs_to_rows(ys_ref, slab_ref, chunks[1])

    y_ssm = jnp.concatenate([slab_ref[s] for s in range(slab_ref.shape[0])], axis=1)
    y = _gelu_tanh(y_ssm + d_ref[...] * u_ref[...])
    t = _dot(y.astype(BF16), w_ref[...]) + b_ref[...]
    o_ref[...] = (y * jax.nn.sigmoid(t)).astype(BF16)


def _pair_specs(n_prompt_tiles, n_pairs, chunks):
    return [pl.BlockSpec((n_pairs, ROW_TILE // chunks[0], chunks[0] * PAIR_CH),
                         lambda i: (0, jnp.minimum(i, n_prompt_tiles - 1), 0)),
            pl.BlockSpec((n_pairs, ROW_TILE // chunks[1], chunks[1] * PAIR_CH), lambda i: (0, 0, 0))]


def _glu(y_p, y_s, u, d_skip, w_b, layer, b, n_prompt_tiles, chunks):
    n, c = u.shape
    row = pl.BlockSpec((ROW_TILE, c), lambda i: (i, 0))
    vec = pl.BlockSpec((1, c), lambda i: (0, 0))
    return pl.pallas_call(
        functools.partial(_glu_kernel, n_prompt_tiles=n_prompt_tiles, chunks=chunks), grid=(n // ROW_TILE,),
        in_specs=_pair_specs(n_prompt_tiles, c // PAIR_CH, chunks)
        + [row, vec, pl.BlockSpec((None, c, c), lambda i: (layer, 0, 0)), vec],
        out_specs=row,
        out_shape=jax.ShapeDtypeStruct((n, c), BF16),
        scratch_shapes=[pltpu.VMEM((c // LANES, ROW_TILE, LANES), F32)],
        compiler_params=_params("arbitrary"), name="glu",
    )(y_p, y_s, u, d_skip.reshape(1, c), w_b, b.reshape(1, c))


def _suffix_matrix(tk):
    j = jnp.arange(tk)[:, None]
    s = jnp.arange(tk + HEAD_DIM)[None, :]
    return ((j > s) | (s >= tk)).astype(BF16)


def _stick_weights(z2, carry, tri, mask, split):
    tk = z2.shape[1]
    neg_abs = lax.bitcast_convert_type(
        lax.bitcast_convert_type(z2, jnp.uint32) | jnp.uint32(0x80000000), F32)
    sp = jnp.maximum(z2, 0.0) + jnp.log2(1.0 + jnp.exp2(neg_abs))
    spm = sp if mask is None else jnp.where(mask, sp, 0.0)
    hi = spm.astype(BF16)
    tot = _dot(hi, tri)
    if split:
        tot = tot + _dot((spm - hi.astype(F32)).astype(BF16), tri)
    w = jnp.exp2((z2 - sp) - tot[:, :tk] - jnp.concatenate([carry] * (tk // HEAD_DIM), axis=1))
    if mask is not None:
        w = jnp.where(mask, w, 0.0)
    return w, carry + tot[:, tk:]


def _attn_prompt_kernel(bias_ref, q_ref, k_ref, v_ref, tri_ref, o_ref, carry_ref, acc_ref):
    hh = pl.program_id(0)
    i = pl.program_id(1)
    tq, tk = ATTN_TQ, ATTN_TK
    bias = bias_ref[hh]
    q = q_ref[...]
    tri = tri_ref[...]

    def tile(j, carry, acc, mask):
        k = k_ref[pl.ds(pl.multiple_of(j * tk, tk), tk), :]
        v = v_ref[pl.ds(pl.multiple_of(j * tk, tk), tk), :]
        w, carry = _stick_weights(_dot_nt(q, k) + bias, carry, tri, mask, split=False)
        return carry, acc + _dot(w.astype(BF16), v)

    def run(j0, count):
        base = pl.multiple_of((j0 - (count - 1)) * tk, tk)
        z = _dot_nt(q, k_ref[pl.ds(base, count * tk), :]) + bias
        z2 = jnp.concatenate([z[:, a * tk:(a + 1) * tk] for a in range(count)], axis=0)
        neg_abs = lax.bitcast_convert_type(
            lax.bitcast_convert_type(z2, jnp.uint32) | jnp.uint32(0x80000000), F32)
        sp = jnp.maximum(z2, 0.0) + jnp.log2(1.0 + jnp.exp2(neg_abs))
        suffix = _dot(sp.astype(BF16), tri[:, :tk])
        row_sum = jnp.sum(sp, axis=1, keepdims=True)
        carry = carry_ref[...]
        carries = [None] * count
        for a in reversed(range(count)):
            carries[a] = jnp.concatenate([carry] * (tk // HEAD_DIM), axis=1)
            carry = carry + row_sum[a * tq:(a + 1) * tq]
        carry_ref[...] = carry
        w = jnp.exp2((z2 - sp) - suffix - jnp.concatenate(carries, axis=0))
        w = jnp.concatenate([w[a * tq:(a + 1) * tq] for a in range(count)], axis=1).astype(BF16)
        acc_ref[...] += _dot(w, v_ref[pl.ds(base, count * tk), :])

    row = lax.broadcasted_iota(jnp.int32, (tq, tk), 0)
    col = lax.broadcasted_iota(jnp.int32, (tq, tk), 1)
    zero = jnp.zeros((tq, HEAD_DIM), F32)
    carry_ref[...], acc_ref[...] = tile(i, zero, zero, col < row)

    group = PROMPT_TILES_PER_BODY
    n_groups = lax.div(i, group)

    def body(n, c):
        run(i - 1 - group * n, group)
        return c

    lax.fori_loop(0, n_groups, body, 0)
    left = i - group * n_groups
    part = group // 2
    while part >= 1:
        @pl.when((left & part) != 0)
        def _(part=part, left=left):
            run((left & (2 * part - 1)) - 1, part)
        part //= 2
    o_ref[...] = acc_ref[...].astype(BF16)


def _attn_prompt(q_b, k_b, v_b, bias2, seq):
    n_heads = q_b.shape[1] // HEAD_DIM
    kv = pl.BlockSpec((seq, HEAD_DIM), lambda h, i: (0, h))
    return pl.pallas_call(
        _attn_prompt_kernel,
        grid=(n_heads, seq // ATTN_TQ),
        in_specs=[pl.BlockSpec(memory_space=pltpu.SMEM),
                  pl.BlockSpec((ATTN_TQ, HEAD_DIM), lambda h, i: (i, h)),
                  kv, kv,
                  pl.BlockSpec((ATTN_TK, ATTN_TK + HEAD_DIM), lambda h, i: (0, 0))],
        out_specs=pl.BlockSpec((ATTN_TQ, HEAD_DIM), lambda h, i: (i, h)),
        out_shape=jax.ShapeDtypeStruct((seq, n_heads * HEAD_DIM), BF16),
        scratch_shapes=[pltpu.VMEM((ATTN_TQ, HEAD_DIM), F32), pltpu.VMEM((ATTN_TQ, HEAD_DIM), F32)],
        compiler_params=_params("arbitrary", "arbitrary"), name="attn_prompt",
    )(bias2, q_b, k_b, v_b, _suffix_matrix(ATTN_TK))


def _attn_sample_kernel(pt_ref, q_ref, kn_ref, vn_ref, *rest, n_heads, n_q, pages_per_step):
    kc_refs = rest[:pages_per_step]
    vc_refs = rest[pages_per_step:2 * pages_per_step]
    bias_ref, tri_ref, o_ref, carry_ref, acc_ref = rest[2 * pages_per_step:]
    p = pl.program_id(1)
    page = kn_ref.shape[0]
    rows = n_heads * n_q
    tri = tri_ref[...]
    bias = bias_ref[...]

    def head_q(hh):
        return q_ref[:, hh * HEAD_DIM:(hh + 1) * HEAD_DIM]

    def sweep(k_of, v_of, mask, carry, acc):
        z = jnp.concatenate([_dot_nt(head_q(hh), k_of(hh)) for hh in range(n_heads)], axis=0)
        w, carry = _stick_weights(z + bias, carry, tri, mask, split=True)
        return carry, acc + jnp.concatenate(
            [_dot(w[hh * n_q:(hh + 1) * n_q].astype(BF16), v_of(hh)) for hh in range(n_heads)], axis=0)

    @pl.when(p == 0)
    def _():
        q_idx = lax.broadcasted_iota(jnp.int32, (rows, page), 0) & (n_q - 1)
        key = lax.broadcasted_iota(jnp.int32, (rows, page), 1)
        zero = jnp.zeros((rows, HEAD_DIM), F32)
        carry_ref[...], acc_ref[...] = sweep(
            lambda hh: kn_ref[:, hh * HEAD_DIM:(hh + 1) * HEAD_DIM],
            lambda hh: vn_ref[:, hh * HEAD_DIM:(hh + 1) * HEAD_DIM], key < q_idx, zero, zero)

    def head_rows(ref, hh):
        return ref[pl.ds(hh, page, stride=n_heads), :].astype(BF16)

    pps = pages_per_step
    z = jnp.concatenate(
        [_dot_nt(head_q(hh), jnp.concatenate([head_rows(r, hh) for r in kc_refs], axis=0))
         for hh in range(n_heads)], axis=0)
    z2 = jnp.concatenate([z[:, r * page:(r + 1) * page] + bias for r in range(pps)], axis=0)
    neg_abs = lax.bitcast_convert_type(
        lax.bitcast_convert_type(z2, jnp.uint32) | jnp.uint32(0x80000000), F32)
    sp = jnp.maximum(z2, 0.0) + jnp.log2(1.0 + jnp.exp2(neg_abs))
    hi = sp.astype(BF16)
    tot = _dot(hi, tri) + _dot((sp - hi.astype(F32)).astype(BF16), tri)
    carry = carry_ref[...]
    carries = []
    for r in range(pps):
        carries.append(carry)
        carry = carry + tot[r * rows:(r + 1) * rows, page:]
    carry_ref[...] = carry
    w = jnp.exp2((z2 - sp) - tot[:, :page] - jnp.concatenate(carries, axis=0))
    acc_ref[...] += jnp.concatenate(
        [_dot(jnp.concatenate([w[r * rows + hh * n_q:r * rows + (hh + 1) * n_q] for r in range(pps)],
                              axis=1).astype(BF16),
              jnp.concatenate([head_rows(r, hh) for r in vc_refs], axis=0))
         for hh in range(n_heads)], axis=0)

    @pl.when(p == pl.num_programs(1) - 1)
    def _():
        acc = acc_ref[...]
        for hh in range(n_heads):
            o_ref[:, hh * HEAD_DIM:(hh + 1) * HEAD_DIM] = acc[hh * n_q:(hh + 1) * n_q]


SAMPLE_PAGES_PER_STEP = 16


def _attn_sample(q_s, k_new, v_new, cache_k, cache_v, layer, page_table, bias2):
    n_seq, n_q, width = q_s.shape
    n_heads = width // HEAD_DIM
    n_layers, n_pool, page = cache_k.shape[:3]
    n_pages = page_table.shape[1]
    pps = SAMPLE_PAGES_PER_STEP
    assert n_q & (n_q - 1) == 0 and n_pages % pps == 0
    rows = n_heads * n_q
    bias_rows = jnp.repeat(bias2, n_q).reshape(rows, 1)
    flat = (n_layers, n_pool, page * n_heads, HEAD_DIM)

    def cache(r):
        return pl.BlockSpec((None, None, page * n_heads, HEAD_DIM),
                            lambda b, p, pt: (layer, pt[b * n_pages + (n_pages - 1 - (p * pps + r))], 0, 0))

    new = pl.BlockSpec((None, page, width), lambda b, p, pt: (b, 0, 0))
    caches = [cache(r) for r in range(pps)]
    return pl.pallas_call(
        functools.partial(_attn_sample_kernel, n_heads=n_heads, n_q=n_q, pages_per_step=pps),
        grid_spec=pltpu.PrefetchScalarGridSpec(
            num_scalar_prefetch=1,
            grid=(n_seq, n_pages // pps),
            in_specs=[pl.BlockSpec((None, n_q, width), lambda b, p, pt: (b, 0, 0)),
                      new, new, *caches, *caches,
                      pl.BlockSpec((rows, 1), lambda b, p, pt: (0, 0)),
                      pl.BlockSpec((page, page + HEAD_DIM), lambda b, p, pt: (0, 0))],
            out_specs=pl.BlockSpec((None, n_q, width), lambda b, p, pt: (b, 0, 0)),
            scratch_shapes=[pltpu.VMEM((rows, HEAD_DIM), F32), pltpu.VMEM((rows, HEAD_DIM), F32)]),
        out_shape=jax.ShapeDtypeStruct((n_seq, n_q, width), F32),
        compiler_params=_params("arbitrary", "arbitrary"), name="attn_sample",
    )(page_table.reshape(-1), q_s, k_new, v_new,
      *([cache_k.reshape(flat)] * pps), *([cache_v.reshape(flat)] * pps), bias_rows, _suffix_matrix(page))


def _merge_kernel(a_ref, b_ref, ga_ref, gb_ref, wa_ref, wb_ref, o_ref):
    m = ga_ref[...] * _dot(a_ref[...], wa_ref[...]) + gb_ref[...] * _dot(b_ref[...], wb_ref[...])
    o_ref[...] = m.astype(BF16)


def _merge(a, b, gates, wa_b, wb_b, layer):
    n, c = a.shape
    d = wa_b.shape[2]
    row = pl.BlockSpec((ROW_TILE, c), lambda i: (i, 0))
    w = pl.BlockSpec((None, c, d), lambda i: (layer, 0, 0))
    return pl.pallas_call(
        _merge_kernel, grid=(n // ROW_TILE,),
        in_specs=[row, row,
                  pl.BlockSpec((ROW_TILE, d), lambda i: (i, 0)),
                  pl.BlockSpec((ROW_TILE, d), lambda i: (i, 1)), w, w],
        out_specs=pl.BlockSpec((ROW_TILE, d), lambda i: (i, 0)),
        out_shape=jax.ShapeDtypeStruct((n, d), BF16),
        compiler_params=_params("arbitrary"), name="merge",
    )(a, b, gates, gates, wa_b, wb_b)


def _route(logits):
    lane_i = lax.broadcasted_iota(jnp.int32, logits.shape, 1)
    lane = lane_i.astype(F32)
    neg = jnp.float32(-jnp.inf)
    big = jnp.float32(ROUTER_LANES)
    is_group = (lane_i >= N_EXPERTS) & (lane_i < N_EXPERTS + MOE_GROUPS)
    g_log = jnp.where(is_group, logits, neg)
    g_max = jnp.max(g_log, axis=-1, keepdims=True)
    g_idx = jnp.min(jnp.where(g_log == g_max, lane, big), axis=-1, keepdims=True) - N_EXPERTS
    g_prob = 1.0 / jnp.sum(jnp.where(is_group, jnp.exp(g_log - g_max), 0.0), axis=-1, keepdims=True)
    lane_group = lax.shift_right_logical(lane_i, 2).astype(F32)
    in_group = (lane_i < N_EXPERTS) & (lane_group == g_idx)
    assert EXPERTS_PER_GROUP == 4
    e_log = jnp.where(in_group, logits, neg)
    v1 = jnp.max(e_log, axis=-1, keepdims=True)
    i1 = jnp.min(jnp.where(e_log == v1, lane, big), axis=-1, keepdims=True)
    e_log2 = jnp.where(lane == i1, neg, e_log)
    v2 = jnp.max(e_log2, axis=-1, keepdims=True)
    i2 = jnp.min(jnp.where(e_log2 == v2, lane, big), axis=-1, keepdims=True)
    e2 = jnp.exp(v2 - v1)
    den = 1.0 + e2
    w1 = g_prob / den
    w2 = g_prob * e2 / den
    return jnp.where(lane == i1, w1, jnp.where(lane == i2, w2, 0.0))


def _out_kernel(m_ref, x_ref, w_ref, g1_ref, gn_ref, sc_ref, sh_ref, wr_hi_ref, wr_lo_ref, br_ref,
                x1_ref, h2_ref, gates_ref):
    x = x_ref[...]
    y3, g13 = _rows_mod(_dot(m_ref[...], w_ref[...]), g1_ref[...])
    x1 = x + (y3 * g13).reshape(x.shape)
    x1_ref[...] = x1
    h2 = _modulated_norm(x1, gn_ref[...], sc_ref[...], sh_ref[...])
    hi = h2.astype(BF16)
    h2_ref[...] = hi
    lo = (h2 - hi.astype(F32)).astype(BF16)
    wr_hi = wr_hi_ref[...]
    logits = _dot(hi, wr_hi) + (_dot(lo, wr_hi) + _dot(hi, wr_lo_ref[...]))
    gates_ref[...] = _route(logits + br_ref[...])


def _out_proj(merged, x, w_out_b, layer, mod3, n_prompt_tiles, norm_g, w_router, b_router):
    n, d = x.shape
    w_router_hi = w_router.astype(BF16)
    row = pl.BlockSpec((ROW_TILE, d), lambda i: (i, 0))
    small = pl.BlockSpec((ROW_TILE, ROUTER_LANES), lambda i: (i, 0))
    return pl.pallas_call(
        _out_kernel, grid=(n // ROW_TILE,),
        in_specs=[row, row,
                  pl.BlockSpec((None, d, d), lambda i: (layer, 0, 0)),
                  _mod_spec(n_prompt_tiles, d, 2),
                  pl.BlockSpec((1, d), lambda i: (0, 0)),
                  _mod_spec(n_prompt_tiles, d, 4),
                  _mod_spec(n_prompt_tiles, d, 3),
                  pl.BlockSpec((d, ROUTER_LANES), lambda i: (0, 0)),
                  pl.BlockSpec((d, ROUTER_LANES), lambda i: (0, 0)),
                  pl.BlockSpec((1, ROUTER_LANES), lambda i: (0, 0))],
        out_specs=[row, row, small],
        out_shape=[jax.ShapeDtypeStruct((n, d), F32), jax.ShapeDtypeStruct((n, d), BF16),
                   jax.ShapeDtypeStruct((n, ROUTER_LANES), F32)],
        compiler_params=_params("arbitrary"), name="out_proj",
    )(merged, x, w_out_b, mod3, norm_g.reshape(1, d), mod3, mod3, w_router_hi,
      (w_router - w_router_hi.astype(F32)).astype(BF16), b_router)


def _moe_kernel(h_ref, gates_ref, wg_ref, wu_ref, wd_ref, o_ref):
    e = pl.program_id(1)
    h = h_ref[...]
    gates = gates_ref[...]
    lane = lax.broadcasted_iota(jnp.int32, gates.shape, 1)
    gate = jnp.sum(jnp.where(lane == e, gates, 0.0), axis=-1, keepdims=True)
    hg = _dot(h, wg_ref[...])
    hu = _dot(h, wu_ref[...])
    act = (hg * jax.nn.sigmoid(hg)) * hu * gate
    contrib = _dot(act.astype(BF16), wd_ref[...])

    @pl.when(e == 0)
    def _():
        o_ref[...] = contrib

    @pl.when(e > 0)
    def _():
        o_ref[...] += contrib


def _moe(h2, gates, wg_b, wu_b, wd_b, layer):
    n, d = h2.shape
    f = wg_b.shape[3]
    n_exp = wg_b.shape[1]
    assert n % MOE_ROW_TILES == 0
    tm = n // MOE_ROW_TILES
    assert tm % 16 == 0
    row = pl.BlockSpec((tm, d), lambda i, e: (i, 0))
    return pl.pallas_call(
        _moe_kernel, grid=(MOE_ROW_TILES, n_exp),
        in_specs=[row,
                  pl.BlockSpec((tm, ROUTER_LANES), lambda i, e: (i, 0)),
                  pl.BlockSpec((None, None, d, f), lambda i, e: (layer, e, 0, 0)),
                  pl.BlockSpec((None, None, d, f), lambda i, e: (layer, e, 0, 0)),
                  pl.BlockSpec((None, None, f, d), lambda i, e: (layer, e, 0, 0))],
        out_specs=row,
        out_shape=jax.ShapeDtypeStruct((n, d), F32),
        compiler_params=_params("arbitrary", "arbitrary"), name="moe",
    )(h2, gates, wg_b, wu_b, wd_b)


def _residual_kernel(x_ref, f_ref, g2_ref, op_ref, os_ref, *, n_prompt_tiles):
    f3, g23 = _rows_mod(f_ref[...], g2_ref[...])
    x = x_ref[...] + (f3 * g23).reshape(x_ref.shape)

    def store(ref):
        ref[...] = x

    _store_split(n_prompt_tiles, op_ref, os_ref, store)


def _residual(x1, ffn, mod3, n_prompt_tiles):
    n, d = x1.shape
    seq = n_prompt_tiles * ROW_TILE
    row = pl.BlockSpec((ROW_TILE, d), lambda i: (i, 0))
    return pl.pallas_call(
        functools.partial(_residual_kernel, n_prompt_tiles=n_prompt_tiles), grid=(n // ROW_TILE,),
        in_specs=[row, row, _mod_spec(n_prompt_tiles, d, 5)],
        out_specs=_split_specs(n_prompt_tiles, (ROW_TILE, d)),
        out_shape=[jax.ShapeDtypeStruct((seq, d), F32), jax.ShapeDtypeStruct((n - seq, d), F32)],
        compiler_params=_params("arbitrary"), name="residual",
    )(x1, ffn, mod3)


def kernel(x_prompt, x_sample, c_prompt, c_sample, cache_k, cache_v, state_ssm_re, state_ssm_im, page_table, w_ada, b_ada, norm_mix_g, w_in, ssm_a_re, ssm_a_im, ssm_log_dt, ssm_b_re, ssm_b_im, ssm_c_re, ssm_c_im, ssm_d, w_glu, b_glu, q_norm_g, k_norm_g, sb_bias, w_branch_ssm, w_branch_attn, w_out, norm_ffn_g, w_router_group, b_router_group, w_router_expert, b_router_expert, w_exp_gate, w_exp_up, w_exp_down):
    n_layers = w_ada.shape[0]
    bsz, seq, d = x_prompt.shape
    n_seq, n_q, _ = x_sample.shape
    assert bsz == 1 and n_q == SUBLANES and n_seq * n_q == ROW_TILE and seq % ROW_TILE == 0
    n_s = n_seq * n_q
    n_prompt_tiles = seq // ROW_TILE
    mod_rows = ROW_TILE // SUBLANES
    ssm_w = ssm_d.shape[1]
    attn_w = w_branch_attn.shape[1]
    assert ssm_w == attn_w
    n_heads = attn_w // HEAD_DIM
    n_groups = ssm_w // SSM_GROUP
    page = cache_k.shape[2]
    chunk_p, chunk_s = 16, n_q
    assert seq % chunk_p == 0 and chunk_p == 2 * chunk_s and page == HEAD_DIM
    chunks = (chunk_p, chunk_s)

    w_in_b = w_in.astype(BF16)
    w_glu_b = w_glu.astype(BF16)
    w_bs_b = w_branch_ssm.astype(BF16)
    w_ba_b = w_branch_attn.astype(BF16)
    w_out_b = w_out.astype(BF16)
    wg_b, wu_b, wd_b = w_exp_gate.astype(BF16), w_exp_up.astype(BF16), w_exp_down.astype(BF16)

    c_all = jnp.concatenate([c_prompt, c_sample], axis=0)
    pad = -c_all.shape[0] % SUBLANES
    mod = _adaln(jnp.pad(c_all, ((0, pad), (0, 0))), w_ada, b_ada)

    x = jnp.concatenate([x_prompt.reshape(seq, d), x_sample.reshape(n_s, d)], axis=0)
    outs = {k: [] for k in ("rp", "ip", "rs", "is")}
    kv_out = None
    for l in range(n_layers):
        mod3_prev = mod3 if l else None
        mod3 = jnp.stack([jnp.broadcast_to(mod[l, 0], (mod_rows, N_MOD * d)), mod[l, 1:1 + n_seq]])
        if l == 0:
            h = _norm_mod(x, norm_mix_g[l], mod3, n_prompt_tiles, 1, 0)
        else:
            x, h = _res_norm_mod(x1, ffn, mod3_prev, norm_mix_g[l], mod3, n_prompt_tiles)
        u, up, us, q_b, k_b, v_b, gates, kv_out = _in_proj(h, w_in_b, l, q_norm_g[l], k_norm_g[l], ssm_w,
                                                           n_prompt_tiles, kv_out, chunks)

        ssm_par = (ssm_a_re[l], ssm_a_im[l], ssm_log_dt[l], ssm_b_re[l], ssm_b_im[l], ssm_c_re[l], ssm_c_im[l])
        prep_p = _ssm_prep(*ssm_par, chunk_p)
        loc_re, loc_im = _ssm_loc(up, prep_p[1], prep_p[2])
        sp_re, sp_im = _ssm_scan(loc_re, loc_im, prep_p[5], prep_p[6])
        yp, np_re, np_im = _ssm_out(up, sp_re, sp_im, prep_p, False)
        h0_re = state_ssm_re[l].reshape(n_seq, n_groups * SSM_STATE)
        h0_im = state_ssm_im[l].reshape(n_seq, n_groups * SSM_STATE)
        ys, ns_re, ns_im = _ssm_out(us, h0_re, h0_im, prep_p, True)
        a_branch = _glu(yp, ys, u, ssm_d[l], w_glu_b, l, b_glu[l], n_prompt_tiles, chunks)

        bias2 = sb_bias[l].astype(F32) * LOG2_E
        bp = _attn_prompt(q_b, k_b, v_b, bias2, seq)
        new_pad = ((0, 0), (0, page - n_q), (0, 0))
        bs = _attn_sample(q_b[seq:].reshape(n_seq, n_q, attn_w),
                          jnp.pad(k_b[seq:].reshape(n_seq, n_q, attn_w), new_pad),
                          jnp.pad(v_b[seq:].reshape(n_seq, n_q, attn_w), new_pad),
                          cache_k, cache_v, l, page_table, bias2)
        b_branch = jnp.concatenate([bp, bs.reshape(n_s, attn_w).astype(BF16)], axis=0)

        merged = _merge(a_branch, b_branch, gates, w_bs_b, w_ba_b, l)
        w_router = jnp.pad(jnp.concatenate([w_router_expert[l], w_router_group[l]], axis=1),
                           ((0, 0), (0, ROUTER_LANES - N_EXPERTS - MOE_GROUPS)))
        b_router = jnp.pad(jnp.concatenate([b_router_expert[l], b_router_group[l]]),
                           (0, ROUTER_LANES - N_EXPERTS - MOE_GROUPS)).reshape(1, ROUTER_LANES)
        x1, h2, route = _out_proj(merged, x, w_out_b, l, mod3, n_prompt_tiles, norm_ffn_g[l], w_router, b_router)
        ffn = _moe(h2, route, wg_b, wu_b, wd_b, l)

        outs["rp"].append(np_re[-1].reshape(1, n_groups, SSM_STATE))
        outs["ip"].append(np_im[-1].reshape(1, n_groups, SSM_STATE))
        outs["rs"].append(ns_re.reshape(n_seq, n_groups, SSM_STATE))
        outs["is"].append(ns_im.reshape(n_seq, n_groups, SSM_STATE))

    y_p, y_s = _residual(x1, ffn, mod3, n_prompt_tiles)
    st = lambda k: jnp.stack(outs[k])
    kp, ks, vp, vs = kv_out
    return (y_p.reshape(1, seq, d), y_s.reshape(n_seq, n_q, d),
            kp.reshape(n_layers, 1, seq, n_heads, HEAD_DIM), vp.reshape(n_layers, 1, seq, n_heads, HEAD_DIM),
            st("rp"), st("ip"),
            ks.reshape(n_layers, n_seq, n_q, n_heads, HEAD_DIM), vs.reshape(n_layers, n_seq, n_q, n_heads, HEAD_DIM),
            st("rs"), st("is"))
```

```python
import functools
import math

import jax
import jax.numpy as jnp
from jax import lax
from jax.experimental import pallas as pl
from jax.experimental.pallas import tpu as pltpu

F32 = jnp.float32
BF16 = jnp.bfloat16

RMS_EPS = 1e-6
HEAD_DIM = 128
SSM_GROUP = 16
SSM_STATE = 64
PAIR_CH = 2 * SSM_GROUP
PAIR_ST = 2 * SSM_STATE
LANES = 128
PAIRS_PER_SLAB = LANES // PAIR_CH
MOE_GROUPS = 4
EXPERTS_PER_GROUP = 4
N_EXPERTS = MOE_GROUPS * EXPERTS_PER_GROUP
ROUTER_LANES = 128
N_MOD = 6
ROW_TILE = 256
SUBLANES = 8
VMEM_LIMIT_BYTES = 56 * 1024 * 1024
ATTN_TQ = 256
ATTN_TK = 256
MOE_TILE = 256
GROUP_LANE = N_EXPERTS
PROMPT_TILES_PER_BODY = 8
LOG2_E = 1.4426950408889634
Q_SCALE_LOG2 = HEAD_DIM ** -0.5 * LOG2_E


def _params(*sem):
    return pltpu.CompilerParams(dimension_semantics=sem, vmem_limit_bytes=VMEM_LIMIT_BYTES)


def _dot(a, b):
    return jnp.dot(a, b, preferred_element_type=F32)


def _dot_nt(a, b):
    return lax.dot_general(a, b, (((1,), (1,)), ((), ())), preferred_element_type=F32)


def _dot_exact(a, b):
    return jnp.dot(a, b, preferred_element_type=F32, precision=lax.Precision.HIGHEST)


def _adaln_kernel(c_ref, w_ref, b_ref, o_ref):
    c = c_ref[...]
    a = (c * jax.nn.sigmoid(c)).astype(BF16)
    o_ref[...] = _dot(a, w_ref[...].astype(BF16)) + b_ref[...]


def _adaln(c_all, w_ada, b_ada):
    n_layers, d, n_out = w_ada.shape
    r = c_all.shape[0]
    tn = 1024
    return pl.pallas_call(
        _adaln_kernel,
        grid=(n_layers, n_out // tn),
        in_specs=[pl.BlockSpec((r, d), lambda l, j: (0, 0)),
                  pl.BlockSpec((None, d, tn), lambda l, j: (l, 0, j)),
                  pl.BlockSpec((None, 1, tn), lambda l, j: (l, 0, j))],
        out_specs=pl.BlockSpec((None, r, tn), lambda l, j: (l, 0, j)),
        out_shape=jax.ShapeDtypeStruct((n_layers, r, n_out), F32),
        compiler_params=_params("arbitrary", "arbitrary"),
        name="adaln",
    )(c_all, w_ada, b_ada.reshape(n_layers, 1, n_out))


def _rows_mod(v, m):
    tm, d = v.shape
    return v.reshape(tm // SUBLANES, SUBLANES, d), m[:, None, :]


def _modulated_norm(x, g, sc, sh):
    ms = jnp.mean(x * x, axis=-1, keepdims=True)
    y = x * lax.rsqrt(ms + RMS_EPS) * g
    y3, sc3 = _rows_mod(y, sc)
    h = y3 * (1.0 + sc3) + sh[:, None, :]
    return h.reshape(x.shape)


def _mod_spec(n_prompt_tiles, d, which):
    return pl.BlockSpec((None, ROW_TILE // SUBLANES, d),
                        lambda i: (jnp.where(i < n_prompt_tiles, 0, 1), 0, which))


def _norm_mod_kernel(x_ref, g_ref, sc_ref, sh_ref, o_ref):
    o_ref[...] = _modulated_norm(x_ref[...], g_ref[...], sc_ref[...], sh_ref[...]).astype(BF16)


def _res_norm_mod_kernel(x1_ref, f_ref, g2_ref, g_ref, sc_ref, sh_ref, x_ref, h_ref):
    f3, g23 = _rows_mod(f_ref[...], g2_ref[...])
    x = x1_ref[...] + (f3 * g23).reshape(x_ref.shape)
    x_ref[...] = x
    h_ref[...] = _modulated_norm(x, g_ref[...], sc_ref[...], sh_ref[...]).astype(BF16)


def _res_norm_mod(x1, ffn, mod3_prev, g, mod3, n_prompt_tiles):
    n, d = x1.shape
    row = pl.BlockSpec((ROW_TILE, d), lambda i: (i, 0))
    return pl.pallas_call(
        _res_norm_mod_kernel,
        grid=(n // ROW_TILE,),
        in_specs=[row, row, _mod_spec(n_prompt_tiles, d, 5),
                  pl.BlockSpec((1, d), lambda i: (0, 0)),
                  _mod_spec(n_prompt_tiles, d, 1), _mod_spec(n_prompt_tiles, d, 0)],
        out_specs=[row, row],
        out_shape=[jax.ShapeDtypeStruct((n, d), F32), jax.ShapeDtypeStruct((n, d), BF16)],
        compiler_params=_params("arbitrary"),
        name="res_norm_mod",
    )(x1, ffn, mod3_prev, g.reshape(1, d), mod3, mod3)


def _norm_mod(x, g, mod3, n_prompt_tiles, sc_idx, sh_idx):
    n, d = x.shape
    return pl.pallas_call(
        _norm_mod_kernel,
        grid=(n // ROW_TILE,),
        in_specs=[pl.BlockSpec((ROW_TILE, d), lambda i: (i, 0)),
                  pl.BlockSpec((1, d), lambda i: (0, 0)),
                  _mod_spec(n_prompt_tiles, d, sc_idx),
                  _mod_spec(n_prompt_tiles, d, sh_idx)],
        out_specs=pl.BlockSpec((ROW_TILE, d), lambda i: (i, 0)),
        out_shape=jax.ShapeDtypeStruct((n, d), BF16),
        compiler_params=_params("arbitrary"),
        name="norm_mod",
    )(x, g.reshape(1, d), mod3, mod3)


def _store_token_major(ref, val):
    tm = val.shape[0]
    n_heads = val.shape[1] // HEAD_DIM
    for hh in range(n_heads):
        ref[pl.ds(hh, tm, stride=n_heads), :] = val[:, hh * HEAD_DIM:(hh + 1) * HEAD_DIM]


def _split_specs(n_prompt_tiles, block, lead=()):
    zeros = (0,) * (len(block) - 1)
    nones = (None,) * len(lead)
    return [pl.BlockSpec(nones + block, lambda i: lead + (jnp.minimum(i, n_prompt_tiles - 1),) + zeros),
            pl.BlockSpec(nones + block, lambda i: lead + (0,) + zeros)]


def _store_split(n_prompt_tiles, prompt_ref, sample_ref, store):
    i = pl.program_id(0)

    @pl.when(i < n_prompt_tiles)
    def _():
        store(prompt_ref)

    @pl.when(i >= n_prompt_tiles)
    def _():
        store(sample_ref)


def _proj_uv_kernel(h_ref, wu_ref, wv_ref, *rest, n_prompt_tiles, chunks):
    u_ref, up_ref, us_ref, vb_ref, vp_ref, vs_ref, slab_ref = rest[-7:]
    h = h_ref[...]
    u = _dot(h, wu_ref[...])
    u_ref[...] = u
    _store_split(n_prompt_tiles, up_ref, us_ref,
                 lambda ref: _rows_to_pairs(slab_ref, u, ref, chunks[0] if ref is up_ref else chunks[1]))
    v = _dot(h, wv_ref[...])
    vb_ref[...] = v.astype(BF16)
    _store_split(n_prompt_tiles, vp_ref, vs_ref, lambda ref: _store_token_major(ref, v))


def _head_norm(acc, g):
    outs = []
    for hh in range(acc.shape[1] // HEAD_DIM):
        blk = acc[:, hh * HEAD_DIM:(hh + 1) * HEAD_DIM]
        ms = jnp.mean(blk * blk, axis=-1, keepdims=True)
        outs.append(blk * lax.rsqrt(ms + RMS_EPS) * g)
    return jnp.concatenate(outs, axis=1)


def _proj_qk_kernel(h_ref, wq_ref, wk_ref, gq_ref, gk_ref, *rest, n_prompt_tiles):
    qb_ref, kb_ref, kp_ref, ks_ref = rest[-4:]
    h = h_ref[...]
    qb_ref[...] = (_head_norm(_dot(h, wq_ref[...]), gq_ref[...]) * Q_SCALE_LOG2).astype(BF16)
    k = _head_norm(_dot(h, wk_ref[...]), gk_ref[...])
    kb_ref[...] = k.astype(BF16)
    _store_split(n_prompt_tiles, kp_ref, ks_ref, lambda ref: _store_token_major(ref, k))


def _proj_gate_kernel(h_ref, w_ref, o_ref):
    o_ref[...] = jax.nn.sigmoid(_dot(h_ref[...], w_ref[...])).astype(BF16)


def _in_proj(h, w_in_b, layer, q_g, k_g, width, n_prompt_tiles, kv_prev, chunks):
    n, d = h.shape
    nt = n // ROW_TILE
    n_layers = w_in_b.shape[0]
    row = pl.BlockSpec((ROW_TILE, d), lambda i: (i, 0))
    out = pl.BlockSpec((ROW_TILE, width), lambda i: (i, 0))

    def wcol(j):
        return pl.BlockSpec((None, d, width), lambda i: (layer, 0, j))

    sds = lambda dt: jax.ShapeDtypeStruct((n, width), dt)
    n_heads = width // HEAD_DIM
    tok_rows = ROW_TILE * n_heads
    tok = _split_specs(n_prompt_tiles, (tok_rows, HEAD_DIM), (layer,))
    tok_sds = [jax.ShapeDtypeStruct((n_layers, n_prompt_tiles * tok_rows, HEAD_DIM), F32),
               jax.ShapeDtypeStruct((n_layers, (nt - n_prompt_tiles) * tok_rows, HEAD_DIM), F32)]
    assert nt - n_prompt_tiles == 1
    prev_k, prev_v = kv_prev[:2], kv_prev[2:]
    prev_specs = [pl.BlockSpec(memory_space=pl.ANY)] * 2
    n_prev = len(prev_specs)
    n_pairs = width // PAIR_CH
    pair_sds = [jax.ShapeDtypeStruct((n_pairs, n_prompt_tiles * ROW_TILE // chunks[0], chunks[0] * PAIR_CH), BF16),
                jax.ShapeDtypeStruct((n_pairs, ROW_TILE // chunks[1], chunks[1] * PAIR_CH), BF16)]
    u, u_p, u_s, v_b, v_p, v_s = pl.pallas_call(
        functools.partial(_proj_uv_kernel, n_prompt_tiles=n_prompt_tiles, chunks=chunks), grid=(nt,),
        in_specs=[row, wcol(0), wcol(3)] + prev_specs,
        out_specs=[out] + _pair_specs(n_prompt_tiles, n_pairs, chunks) + [out] + tok,
        out_shape=[sds(F32)] + pair_sds + [sds(BF16)] + tok_sds,
        input_output_aliases={3 + j: 4 + j for j in range(n_prev)},
        scratch_shapes=[pltpu.VMEM((width // LANES, ROW_TILE, LANES), F32)],
        compiler_params=_params("arbitrary"), name="proj_uv",
    )(h, w_in_b, w_in_b, *prev_v)
    gspec = pl.BlockSpec((1, HEAD_DIM), lambda i: (0, 0))
    q_b, k_b, k_p, k_s = pl.pallas_call(
        functools.partial(_proj_qk_kernel, n_prompt_tiles=n_prompt_tiles), grid=(nt,),
        in_specs=[row, wcol(1), wcol(2), gspec, gspec] + prev_specs, out_specs=[out, out] + tok,
        out_shape=[sds(BF16), sds(BF16)] + tok_sds,
        input_output_aliases={5 + j: 2 + j for j in range(n_prev)},
        compiler_params=_params("arbitrary"), name="proj_qk",
    )(h, w_in_b, w_in_b, q_g.reshape(1, HEAD_DIM), k_g.reshape(1, HEAD_DIM), *prev_k)
    n_gate = (w_in_b.shape[2] - 4 * width) // width
    gates = pl.pallas_call(
        _proj_gate_kernel, grid=(n_gate, nt),
        in_specs=[pl.BlockSpec((ROW_TILE, d), lambda j, i: (i, 0)),
                  pl.BlockSpec((None, d, width), lambda j, i: (layer, 0, 4 + j))],
        out_specs=pl.BlockSpec((ROW_TILE, width), lambda j, i: (i, j)),
        out_shape=jax.ShapeDtypeStruct((n, n_gate * width), BF16),
        compiler_params=_params("arbitrary", "arbitrary"), name="proj_gate",
    )(h, w_in_b)
    return u, u_p, u_s, q_b, k_b, v_b, gates, (k_p, k_s, v_p, v_s)


def _cmul(ar, ai, br, bi):
    return ar * br - ai * bi, ar * bi + ai * br


def _discretise(a_re, a_im, log_dt):
    dt = jnp.exp(log_dt)
    mag = jnp.exp(dt * a_re)
    ab_re = mag * jnp.cos(dt * a_im)
    ab_im = mag * jnp.sin(dt * a_im)
    den = a_re * a_re + a_im * a_im
    f_re = ((ab_re - 1.0) * a_re + ab_im * a_im) / den
    f_im = (ab_im * a_re - (ab_re - 1.0) * a_im) / den
    return dt, f_re, f_im


def _abar_pow(a_re, a_im, dt, k):
    mag = jnp.exp(k * dt * a_re)
    th = k * dt * a_im
    return mag * jnp.cos(th), mag * jnp.sin(th)


def _ssm_prep_kernel(as_re_ref, as_im_ref, ldts_ref, al_re_ref, al_im_ref, ldtl_ref,
                     bt_re_ref, bt_im_ref, btr_re_ref, btr_im_ref, c_re_ref, c_im_ref,
                     ct_re_ref, ct_im_ref,
                     m_ref, e_re_ref, e_im_ref, f_re_ref, f_im_ref, p_re_ref, p_im_ref, *, chunk):
    width = chunk * PAIR_CH
    as_re, as_im = as_re_ref[...], as_im_ref[...]
    dts, fs_re, fs_im = _discretise(as_re, as_im, ldts_ref[...])
    al_re, al_im = al_re_ref[...], al_im_ref[...]
    dtl, fl_re, fl_im = _discretise(al_re, al_im, ldtl_ref[...])

    lane_blk = lax.shift_right_logical(lax.broadcasted_iota(jnp.int32, (1, width), 1), 5)
    assert PAIR_CH == 32

    k_lane = lax.broadcasted_iota(jnp.int32, (1, LANES), 1)
    assert chunk < LANES
    pw_re, pw_im = _abar_pow(as_re, as_im, dts, jnp.minimum(k_lane, chunk).astype(F32))
    k_row = lax.broadcasted_iota(jnp.int32, (LANES, width), 0)

    def spread(lane_to_k):
        onehot = (k_row == lane_to_k).astype(F32)
        return _dot_exact(pw_re, onehot), _dot_exact(pw_im, onehot)

    pk_re, pk_im = spread(chunk - 1 - lane_blk)
    bb_re, bb_im = _cmul(fs_re, fs_im, bt_re_ref[...], bt_im_ref[...])
    w_re, w_im = _cmul(pk_re, pk_im, bb_re, bb_im)
    kr = _dot_exact(c_re_ref[...], w_re) - _dot_exact(c_im_ref[...], w_im)
    lane = lax.broadcasted_iota(jnp.int32, (PAIR_CH, width), 1)
    for t in range(chunk):
        shift = (chunk - 1 - t) * PAIR_CH
        rolled = kr if shift == 0 else pltpu.roll(kr, width - shift, axis=1)
        m_ref[t * PAIR_CH:(t + 1) * PAIR_CH, :] = jnp.where(lane < (t + 1) * PAIR_CH, rolled, 0.0)

    bbt_re, bbt_im = _cmul(fl_re, fl_im, btr_re_ref[...], btr_im_ref[...])
    for tau in range(chunk):
        q_re, q_im = _abar_pow(al_re, al_im, dtl, float(chunk - 1 - tau))
        e_re, e_im = _cmul(q_re, q_im, bbt_re, bbt_im)
        e_re_ref[tau * PAIR_CH:(tau + 1) * PAIR_CH, :] = e_re
        e_im_ref[tau * PAIR_CH:(tau + 1) * PAIR_CH, :] = e_im

    pf_re, pf_im = spread(lane_blk + 1)
    ca_re, ca_im = _cmul(ct_re_ref[...], ct_im_ref[...], pf_re, pf_im)
    f_re_ref[...] = ca_re
    f_im_ref[...] = -ca_im

    pt_re, pt_im = _abar_pow(al_re, al_im, dtl, float(chunk))
    ph_re, ph_im = _abar_pow(al_re, al_im, dtl, float(chunk // 2))
    p_re_ref[...] = jnp.concatenate([pt_re, ph_re], axis=0)
    p_im_ref[...] = jnp.concatenate([pt_im, ph_im], axis=0)


def _ssm_layouts(a_re, a_im, log_dt, b_re, b_im, c_re, c_im, chunk):
    n_groups = a_re.shape[0]
    n_pairs = n_groups // 2
    eye = jnp.eye(2, dtype=F32)

    def sub(x):
        return x.reshape(n_pairs, PAIR_ST, 1)

    def lan(x):
        return x.reshape(n_pairs, 1, PAIR_ST)

    ldt = jnp.broadcast_to(log_dt[:, None], (n_groups, SSM_STATE))

    def b_tiled(b):
        bp = b.reshape(n_pairs, 2, SSM_STATE, SSM_GROUP)
        bd = jnp.einsum('ngph,gk->ngpkh', bp, eye).reshape(n_pairs, PAIR_ST, PAIR_CH)
        return jnp.tile(bd, (1, 1, chunk))

    def b_rows(b):
        bp = b.reshape(n_pairs, 2, SSM_STATE, SSM_GROUP)
        return jnp.einsum('ngph,gk->nghkp', bp, eye).reshape(n_pairs, PAIR_CH, PAIR_ST)

    def c_rows(c):
        cp = c.reshape(n_pairs, 2, SSM_GROUP, SSM_STATE)
        return jnp.einsum('nghp,gk->nghkp', cp, eye).reshape(n_pairs, PAIR_CH, PAIR_ST)

    def c_tiled(c):
        cp = c.reshape(n_pairs, 2, SSM_GROUP, SSM_STATE)
        cd = jnp.einsum('nghp,gk->nkpgh', cp, eye).reshape(n_pairs, PAIR_ST, PAIR_CH)
        return jnp.tile(cd, (1, 1, chunk))

    return (sub(a_re), sub(a_im), sub(ldt), lan(a_re), lan(a_im), lan(ldt),
            b_tiled(b_re), b_tiled(b_im), b_rows(b_re), b_rows(b_im),
            c_rows(c_re), c_rows(c_im), c_tiled(c_re), c_tiled(c_im))


def _ssm_prep(a_re, a_im, log_dt, b_re, b_im, c_re, c_im, chunk):
    n_pairs = a_re.shape[0] // 2
    width = chunk * PAIR_CH
    args = _ssm_layouts(a_re, a_im, log_dt, b_re, b_im, c_re, c_im, chunk)

    def spec(r, c):
        return pl.BlockSpec((None, r, c), lambda i: (i, 0, 0))

    sub, lan = spec(PAIR_ST, 1), spec(1, PAIR_ST)
    wide, rows = spec(PAIR_ST, width), spec(PAIR_CH, PAIR_ST)
    sds = lambda r, c: jax.ShapeDtypeStruct((n_pairs, r, c), F32)
    return pl.pallas_call(
        functools.partial(_ssm_prep_kernel, chunk=chunk),
        grid=(n_pairs,),
        in_specs=[sub, sub, sub, lan, lan, lan, wide, wide, rows, rows, rows, rows, wide, wide],
        out_specs=[spec(width, width), spec(width, PAIR_ST), spec(width, PAIR_ST),
                   wide, wide, spec(2, PAIR_ST), spec(2, PAIR_ST)],
        out_shape=[sds(width, width), sds(width, PAIR_ST), sds(width, PAIR_ST),
                   sds(PAIR_ST, width), sds(PAIR_ST, width), sds(2, PAIR_ST), sds(2, PAIR_ST)],
        compiler_params=_params("arbitrary"),
        name="ssm_prep",
    )(*args)


def _ssm_loc_kernel(u_ref, e_re_ref, e_im_ref, l_re_ref, l_im_ref):
    ub = u_ref[...]
    l_re_ref[...] = _dot(ub, e_re_ref[...].astype(BF16))
    l_im_ref[...] = _dot(ub, e_im_ref[...].astype(BF16))


def _ssm_loc(u_pairs, e_re, e_im):
    n_pairs, rows, width = u_pairs.shape
    st = pl.BlockSpec((rows, PAIR_ST), lambda i: (0, i))
    sds = jax.ShapeDtypeStruct((rows, n_pairs * PAIR_ST), F32)
    espec = pl.BlockSpec((None, width, PAIR_ST), lambda i: (i, 0, 0))
    return pl.pallas_call(
        _ssm_loc_kernel, grid=(n_pairs,),
        in_specs=[pl.BlockSpec((None, rows, width), lambda i: (i, 0, 0)), espec, espec],
        out_specs=[st, st], out_shape=[sds, sds],
        compiler_params=_params("arbitrary"), name="ssm_loc",
    )(u_pairs, e_re, e_im)


def _ssm_scan_kernel(l_re_ref, l_im_ref, p_re_ref, p_im_ref, s_re_ref, s_im_ref):
    p_re, p_im = p_re_ref[...], p_im_ref[...]
    n_chunks = l_re_ref.shape[0]

    def body(c, s):
        s_re, s_im = s
        s_re_ref[pl.ds(c, 1), :] = s_re
        s_im_ref[pl.ds(c, 1), :] = s_im
        n_re, n_im = _cmul(p_re, p_im, s_re, s_im)
        return n_re + l_re_ref[pl.ds(c, 1), :], n_im + l_im_ref[pl.ds(c, 1), :]

    zero = jnp.zeros_like(p_re)
    lax.fori_loop(0, n_chunks, body, (zero, zero))


def _ssm_scan(loc_re, loc_im, p_re, p_im):
    rows, n_state = loc_re.shape
    tc = 1024
    blk = pl.BlockSpec((rows, tc), lambda j: (0, j))
    par = pl.BlockSpec((1, tc), lambda j: (0, j))
    sds = jax.ShapeDtypeStruct((rows, n_state), F32)
    return pl.pallas_call(
        _ssm_scan_kernel, grid=(n_state // tc,),
        in_specs=[blk, blk, par, par], out_specs=[blk, blk], out_shape=[sds, sds],
        compiler_params=_params("arbitrary"), name="ssm_scan",
    )(loc_re, loc_im, p_re[:, 0].reshape(1, n_state), p_im[:, 0].reshape(1, n_state))


def _gelu_tanh(x):
    c = math.sqrt(2.0 / math.pi)
    return 0.5 * x * (1.0 + jnp.tanh(c * (x + 0.044715 * (x * x * x))))


def _ssm_out_kernel(u_ref, s_re_ref, s_im_ref, m_ref, e_re_ref, e_im_ref, f_re_ref, f_im_ref,
                    p_re_ref, p_im_ref, y_ref, n_re_ref, n_im_ref):
    ub = u_ref[...]
    s_re, s_im = s_re_ref[...], s_im_ref[...]
    y = (_dot_nt(ub, m_ref[...].astype(BF16))
         + _dot(s_re.astype(BF16), f_re_ref[...].astype(BF16))
         + _dot(s_im.astype(BF16), f_im_ref[...].astype(BF16)))
    y_ref[...] = y.astype(BF16)
    a_re, a_im = _cmul(p_re_ref[...], p_im_ref[...], s_re, s_im)
    n_re_ref[...] = a_re + _dot(ub, e_re_ref[...].astype(BF16))
    n_im_ref[...] = a_im + _dot(ub, e_im_ref[...].astype(BF16))


def _ssm_out(u_pairs, s_re, s_im, prep, half):
    n_pairs, rows, width = u_pairs.shape
    m, e_re, e_im, f_re, f_im, p_re, p_im = prep
    st = pl.BlockSpec((rows, PAIR_ST), lambda i: (0, i))
    per = lambda r, c: pl.BlockSpec((None, r, c), lambda i: (i, 0, 0))
    e_spec = pl.BlockSpec((None, width, PAIR_ST), lambda i: (i, 1 if half else 0, 0))
    p_spec = pl.BlockSpec((None, 1, PAIR_ST), lambda i: (i, 0, 0))
    sds = jax.ShapeDtypeStruct((rows, n_pairs * PAIR_ST), F32)
    pick = slice(1, 2) if half else slice(0, 1)
    return pl.pallas_call(
        _ssm_out_kernel, grid=(n_pairs,),
        in_specs=[per(rows, width), st, st, per(width, width), e_spec, e_spec,
                  per(PAIR_ST, width), per(PAIR_ST, width), p_spec, p_spec],
        out_specs=[per(rows, width), st, st],
        out_shape=[jax.ShapeDtypeStruct((n_pairs, rows, width), BF16), sds, sds],
        compiler_params=_params("arbitrary"), name="ssm_out",
    )(u_pairs, s_re, s_im, m, e_re, e_im, f_re, f_im, p_re[:, pick], p_im[:, pick])


def _rows_to_pairs(slab_ref, val, out_ref, chunk):
    tm, c = val.shape
    rows = tm // chunk
    for s in range(c // LANES):
        slab_ref[s] = val[:, s * LANES:(s + 1) * LANES]
    for step in range(chunk):
        dst = (step % PAIRS_PER_SLAB) * PAIR_CH
        tile0 = (step // PAIRS_PER_SLAB) * LANES
        for s in range(c // LANES):
            x = slab_ref[s, pl.ds(step, rows, stride=chunk), :]
            for j in range(PAIRS_PER_SLAB):
                shift = (dst - j * PAIR_CH) % LANES
                moved = x if shift == 0 else pltpu.roll(x, shift, axis=1)
                out_ref[s * PAIRS_PER_SLAB + j, :, tile0 + dst:tile0 + dst + PAIR_CH] = (
                    moved[:, dst:dst + PAIR_CH].astype(BF16))


def _pairs_to_rows(in_ref, slab_ref, chunk):
    n_pairs, rows, _ = in_ref.shape
    lane_pair = lax.shift_right_logical(lax.broadcasted_iota(jnp.int32, (rows, LANES), 1), 5)
    assert PAIR_CH == 32
    for step in range(chunk):
        src = (step % PAIRS_PER_SLAB) * PAIR_CH
        tile0 = (step // PAIRS_PER_SLAB) * LANES
        for s in range(n_pairs // PAIRS_PER_SLAB):
            acc = None
            for j in range(PAIRS_PER_SLAB):
                x = in_ref[s * PAIRS_PER_SLAB + j, :, tile0:tile0 + LANES].astype(F32)
                shift = (j * PAIR_CH - src) % LANES
                moved = x if shift == 0 else pltpu.roll(x, shift, axis=1)
                acc = moved if acc is None else jnp.where(lane_pair == j, moved, acc)
            slab_ref[s, pl.ds(step, rows, stride=chunk), :] = acc


def _glu_kernel(yp_ref, ys_ref, u_ref, d_ref, w_ref, b_ref, o_ref, slab_ref, *, n_prompt_tiles, chunks):
    i = pl.program_id(0)

    @pl.when(i < n_prompt_tiles)
    def _():
        _pairs_to_rows(yp_ref, slab_ref, chunks[0])

    @pl.when(i >= n_prompt_tiles)
    def _():
        _pairs_to_rows(ys_ref, slab_ref, chunks[1])

    y_ssm = jnp.concatenate([slab_ref[s] for s in range(slab_ref.shape[0])], axis=1)
    y = _gelu_tanh(y_ssm + d_ref[...] * u_ref[...])
    t = _dot(y.astype(BF16), w_ref[...]) + b_ref[...]
    o_ref[...] = (y * jax.nn.sigmoid(t)).astype(BF16)


def _pair_specs(n_prompt_tiles, n_pairs, chunks):
    return [pl.BlockSpec((n_pairs, ROW_TILE // chunks[0], chunks[0] * PAIR_CH),
                         lambda i: (0, jnp.minimum(i, n_prompt_tiles - 1), 0)),
            pl.BlockSpec((n_pairs, ROW_TILE // chunks[1], chunks[1] * PAIR_CH), lambda i: (0, 0, 0))]


def _glu(y_p, y_s, u, d_skip, w_b, layer, b, n_prompt_tiles, chunks):
    n, c = u.shape
    row = pl.BlockSpec((ROW_TILE, c), lambda i: (i, 0))
    vec = pl.BlockSpec((1, c), lambda i: (0, 0))
    return pl.pallas_call(
        functools.partial(_glu_kernel, n_prompt_tiles=n_prompt_tiles, chunks=chunks), grid=(n // ROW_TILE,),
        in_specs=_pair_specs(n_prompt_tiles, c // PAIR_CH, chunks)
        + [row, vec, pl.BlockSpec((None, c, c), lambda i: (layer, 0, 0)), vec],
        out_specs=row,
        out_shape=jax.ShapeDtypeStruct((n, c), BF16),
        scratch_shapes=[pltpu.VMEM((c // LANES, ROW_TILE, LANES), F32)],
        compiler_params=_params("arbitrary"), name="glu",
    )(y_p, y_s, u, d_skip.reshape(1, c), w_b, b.reshape(1, c))


def _suffix_matrix(tk):
    j = jnp.arange(tk)[:, None]
    s = jnp.arange(tk + HEAD_DIM)[None, :]
    return ((j > s) | (s >= tk)).astype(BF16)


def _stick_weights(z2, carry, tri, mask, split):
    tk = z2.shape[1]
    neg_abs = lax.bitcast_convert_type(
        lax.bitcast_convert_type(z2, jnp.uint32) | jnp.uint32(0x80000000), F32)
    sp = jnp.maximum(z2, 0.0) + jnp.log2(1.0 + jnp.exp2(neg_abs))
    spm = sp if mask is None else jnp.where(mask, sp, 0.0)
    hi = spm.astype(BF16)
    tot = _dot(hi, tri)
    if split:
        tot = tot + _dot((spm - hi.astype(F32)).astype(BF16), tri)
    w = jnp.exp2((z2 - sp) - tot[:, :tk] - jnp.concatenate([carry] * (tk // HEAD_DIM), axis=1))
    if mask is not None:
        w = jnp.where(mask, w, 0.0)
    return w, carry + tot[:, tk:]


def _attn_prompt_kernel(bias_ref, q_ref, k_ref, v_ref, tri_ref, o_ref, carry_ref, acc_ref):
    hh = pl.program_id(0)
    i = pl.program_id(1)
    tq, tk = ATTN_TQ, ATTN_TK
    bias = bias_ref[hh]
    q = q_ref[...]
    tri = tri_ref[...]

    def tile(j, carry, acc, mask):
        k = k_ref[pl.ds(pl.multiple_of(j * tk, tk), tk), :]
        v = v_ref[pl.ds(pl.multiple_of(j * tk, tk), tk), :]
        w, carry = _stick_weights(_dot_nt(q, k) + bias, carry, tri, mask, split=False)
        return carry, acc + _dot(w.astype(BF16), v)

    def run(j0, count):
        base = pl.multiple_of((j0 - (count - 1)) * tk, tk)
        z = _dot_nt(q, k_ref[pl.ds(base, count * tk), :]) + bias
        z2 = jnp.concatenate([z[:, a * tk:(a + 1) * tk] for a in range(count)], axis=0)
        neg_abs = lax.bitcast_convert_type(
            lax.bitcast_convert_type(z2, jnp.uint32) | jnp.uint32(0x80000000), F32)
        sp = jnp.maximum(z2, 0.0) + jnp.log2(1.0 + jnp.exp2(neg_abs))
        suffix = _dot(sp.astype(BF16), tri[:, :tk])
        row_sum = jnp.sum(sp, axis=1, keepdims=True)
        carry = carry_ref[...]
        carries = [None] * count
        for a in reversed(range(count)):
            carries[a] = jnp.concatenate([carry] * (tk // HEAD_DIM), axis=1)
            carry = carry + row_sum[a * tq:(a + 1) * tq]
        carry_ref[...] = carry
        w = jnp.exp2((z2 - sp) - suffix - jnp.concatenate(carries, axis=0))
        w = jnp.concatenate([w[a * tq:(a + 1) * tq] for a in range(count)], axis=1).astype(BF16)
        acc_ref[...] += _dot(w, v_ref[pl.ds(base, count * tk), :])

    row = lax.broadcasted_iota(jnp.int32, (tq, tk), 0)
    col = lax.broadcasted_iota(jnp.int32, (tq, tk), 1)
    zero = jnp.zeros((tq, HEAD_DIM), F32)
    carry_ref[...], acc_ref[...] = tile(i, zero, zero, col < row)

    group = PROMPT_TILES_PER_BODY
    n_groups = lax.div(i, group)

    def body(n, c):
        run(i - 1 - group * n, group)
        return c

    lax.fori_loop(0, n_groups, body, 0)
    left = i - group * n_groups
    part = group // 2
    while part >= 1:
        @pl.when((left & part) != 0)
        def _(part=part, left=left):
            run((left & (2 * part - 1)) - 1, part)
        part //= 2
    o_ref[...] = acc_ref[...].astype(BF16)


def _attn_prompt(q_b, k_b, v_b, bias2, seq):
    n_heads = q_b.shape[1] // HEAD_DIM
    kv = pl.BlockSpec((seq, HEAD_DIM), lambda h, i: (0, h))
    return pl.pallas_call(
        _attn_prompt_kernel,
        grid=(n_heads, seq // ATTN_TQ),
        in_specs=[pl.BlockSpec(memory_space=pltpu.SMEM),
                  pl.BlockSpec((ATTN_TQ, HEAD_DIM), lambda h, i: (i, h)),
                  kv, kv,
                  pl.BlockSpec((ATTN_TK, ATTN_TK + HEAD_DIM), lambda h, i: (0, 0))],
        out_specs=pl.BlockSpec((ATTN_TQ, HEAD_DIM), lambda h, i: (i, h)),
        out_shape=jax.ShapeDtypeStruct((seq, n_heads * HEAD_DIM), BF16),
        scratch_shapes=[pltpu.VMEM((ATTN_TQ, HEAD_DIM), F32), pltpu.VMEM((ATTN_TQ, HEAD_DIM), F32)],
        compiler_params=_params("arbitrary", "arbitrary"), name="attn_prompt",
    )(bias2, q_b, k_b, v_b, _suffix_matrix(ATTN_TK))


def _attn_sample_kernel(pt_ref, q_ref, kn_ref, vn_ref, *rest, n_heads, n_q, pages_per_step):
    kc_refs = rest[:pages_per_step]
    vc_refs = rest[pages_per_step:2 * pages_per_step]
    bias_ref, tri_ref, o_ref, carry_ref, acc_ref = rest[2 * pages_per_step:]
    p = pl.program_id(1)
    page = kn_ref.shape[0]
    rows = n_heads * n_q
    tri = tri_ref[...]
    bias = bias_ref[...]

    def head_q(hh):
        return q_ref[:, hh * HEAD_DIM:(hh + 1) * HEAD_DIM]

    def sweep(k_of, v_of, mask, carry, acc):
        z = jnp.concatenate([_dot_nt(head_q(hh), k_of(hh)) for hh in range(n_heads)], axis=0)
        w, carry = _stick_weights(z + bias, carry, tri, mask, split=True)
        return carry, acc + jnp.concatenate(
            [_dot(w[hh * n_q:(hh + 1) * n_q].astype(BF16), v_of(hh)) for hh in range(n_heads)], axis=0)

    @pl.when(p == 0)
    def _():
        q_idx = lax.broadcasted_iota(jnp.int32, (rows, page), 0) & (n_q - 1)
        key = lax.broadcasted_iota(jnp.int32, (rows, page), 1)
        zero = jnp.zeros((rows, HEAD_DIM), F32)
        carry_ref[...], acc_ref[...] = sweep(
            lambda hh: kn_ref[:, hh * HEAD_DIM:(hh + 1) * HEAD_DIM],
            lambda hh: vn_ref[:, hh * HEAD_DIM:(hh + 1) * HEAD_DIM], key < q_idx, zero, zero)

    def head_rows(ref, hh):
        return ref[pl.ds(hh, page, stride=n_heads), :].astype(BF16)

    pps = pages_per_step
    z = jnp.concatenate(
        [_dot_nt(head_q(hh), jnp.concatenate([head_rows(r, hh) for r in kc_refs], axis=0))
         for hh in range(n_heads)], axis=0)
    z2 = jnp.concatenate([z[:, r * page:(r + 1) * page] + bias for r in range(pps)], axis=0)
    neg_abs = lax.bitcast_convert_type(
        lax.bitcast_convert_type(z2, jnp.uint32) | jnp.uint32(0x80000000), F32)
    sp = jnp.maximum(z2, 0.0) + jnp.log2(1.0 + jnp.exp2(neg_abs))
    hi = sp.astype(BF16)
    tot = _dot(hi, tri) + _dot((sp - hi.astype(F32)).astype(BF16), tri)
    carry = carry_ref[...]
    carries = []
    for r in range(pps):
        carries.append(carry)
        carry = carry + tot[r * rows:(r + 1) * rows, page:]
    carry_ref[...] = carry
    w = jnp.exp2((z2 - sp) - tot[:, :page] - jnp.concatenate(carries, axis=0))
    acc_ref[...] += jnp.concatenate(
        [_dot(jnp.concatenate([w[r * rows + hh * n_q:r * rows + (hh + 1) * n_q] for r in range(pps)],
                              axis=1).astype(BF16),
              jnp.concatenate([head_rows(r, hh) for r in vc_refs], axis=0))
         for hh in range(n_heads)], axis=0)

    @pl.when(p == pl.num_programs(1) - 1)
    def _():
        acc = acc_ref[...]
        for hh in range(n_heads):
            o_ref[:, hh * HEAD_DIM:(hh + 1) * HEAD_DIM] = acc[hh * n_q:(hh + 1) * n_q]


SAMPLE_PAGES_PER_STEP = 16


def _attn_sample(q_s, k_new, v_new, cache_k, cache_v, layer, page_table, bias2):
    n_seq, n_q, width = q_s.shape
    n_heads = width // HEAD_DIM
    n_layers, n_pool, page = cache_k.shape[:3]
    n_pages = page_table.shape[1]
    pps = SAMPLE_PAGES_PER_STEP
    assert n_q & (n_q - 1) == 0 and n_pages % pps == 0
    rows = n_heads * n_q
    bias_rows = jnp.repeat(bias2, n_q).reshape(rows, 1)
    flat = (n_layers, n_pool, page * n_heads, HEAD_DIM)

    def cache(r):
        return pl.BlockSpec((None, None, page * n_heads, HEAD_DIM),
                            lambda b, p, pt: (layer, pt[b * n_pages + (n_pages - 1 - (p * pps + r))], 0, 0))

    new = pl.BlockSpec((None, page, width), lambda b, p, pt: (b, 0, 0))
    caches = [cache(r) for r in range(pps)]
    return pl.pallas_call(
        functools.partial(_attn_sample_kernel, n_heads=n_heads, n_q=n_q, pages_per_step=pps),
        grid_spec=pltpu.PrefetchScalarGridSpec(
            num_scalar_prefetch=1,
            grid=(n_seq, n_pages // pps),
            in_specs=[pl.BlockSpec((None, n_q, width), lambda b, p, pt: (b, 0, 0)),
                      new, new, *caches, *caches,
                      pl.BlockSpec((rows, 1), lambda b, p, pt: (0, 0)),
                      pl.BlockSpec((page, page + HEAD_DIM), lambda b, p, pt: (0, 0))],
            out_specs=pl.BlockSpec((None, n_q, width), lambda b, p, pt: (b, 0, 0)),
            scratch_shapes=[pltpu.VMEM((rows, HEAD_DIM), F32), pltpu.VMEM((rows, HEAD_DIM), F32)]),
        out_shape=jax.ShapeDtypeStruct((n_seq, n_q, width), F32),
        compiler_params=_params("arbitrary", "arbitrary"), name="attn_sample",
    )(page_table.reshape(-1), q_s, k_new, v_new,
      *([cache_k.reshape(flat)] * pps), *([cache_v.reshape(flat)] * pps), bias_rows, _suffix_matrix(page))


def _merge_kernel(a_ref, b_ref, ga_ref, gb_ref, wa_ref, wb_ref, o_ref):
    m = (ga_ref[...].astype(F32) * _dot(a_ref[...], wa_ref[...])
         + gb_ref[...].astype(F32) * _dot(b_ref[...], wb_ref[...]))
    o_ref[...] = m.astype(BF16)


def _merge(a, b, gates, wa_b, wb_b, layer):
    n, c = a.shape
    d = wa_b.shape[2]
    row = pl.BlockSpec((ROW_TILE, c), lambda i: (i, 0))
    w = pl.BlockSpec((None, c, d), lambda i: (layer, 0, 0))
    return pl.pallas_call(
        _merge_kernel, grid=(n // ROW_TILE,),
        in_specs=[row, row,
                  pl.BlockSpec((ROW_TILE, d), lambda i: (i, 0)),
                  pl.BlockSpec((ROW_TILE, d), lambda i: (i, 1)), w, w],
        out_specs=pl.BlockSpec((ROW_TILE, d), lambda i: (i, 0)),
        out_shape=jax.ShapeDtypeStruct((n, d), BF16),
        compiler_params=_params("arbitrary"), name="merge",
    )(a, b, gates, gates, wa_b, wb_b)


def _route(logits):
    lane_i = lax.broadcasted_iota(jnp.int32, logits.shape, 1)
    lane = lane_i.astype(F32)
    neg = jnp.float32(-jnp.inf)
    big = jnp.float32(ROUTER_LANES)
    is_group = (lane_i >= N_EXPERTS) & (lane_i < N_EXPERTS + MOE_GROUPS)
    g_log = jnp.where(is_group, logits, neg)
    g_max = jnp.max(g_log, axis=-1, keepdims=True)
    g_idx = jnp.min(jnp.where(g_log == g_max, lane, big), axis=-1, keepdims=True) - N_EXPERTS
    g_prob = 1.0 / jnp.sum(jnp.where(is_group, jnp.exp(g_log - g_max), 0.0), axis=-1, keepdims=True)
    lane_group = lax.shift_right_logical(lane_i, 2).astype(F32)
    in_group = (lane_i < N_EXPERTS) & (lane_group == g_idx)
    assert EXPERTS_PER_GROUP == 4
    e_log = jnp.where(in_group, logits, neg)
    v1 = jnp.max(e_log, axis=-1, keepdims=True)
    i1 = jnp.min(jnp.where(e_log == v1, lane, big), axis=-1, keepdims=True)
    e_log2 = jnp.where(lane == i1, neg, e_log)
    v2 = jnp.max(e_log2, axis=-1, keepdims=True)
    i2 = jnp.min(jnp.where(e_log2 == v2, lane, big), axis=-1, keepdims=True)
    e2 = jnp.exp(v2 - v1)
    den = 1.0 + e2
    w1 = g_prob / den
    w2 = g_prob * e2 / den
    gates = jnp.where(lane == i1, w1, jnp.where(lane == i2, w2, 0.0))
    return jnp.where(lane_i == GROUP_LANE, g_idx, gates)


def _out_kernel(m_ref, x_ref, w_ref, g1_ref, gn_ref, sc_ref, sh_ref, wr_hi_ref, wr_lo_ref, br_ref,
                x1_ref, h2_ref, gates_ref):
    x = x_ref[...]
    y3, g13 = _rows_mod(_dot(m_ref[...], w_ref[...]), g1_ref[...])
    x1 = x + (y3 * g13).reshape(x.shape)
    x1_ref[...] = x1
    h2 = _modulated_norm(x1, gn_ref[...], sc_ref[...], sh_ref[...])
    hi = h2.astype(BF16)
    h2_ref[...] = h2
    lo = (h2 - hi.astype(F32)).astype(BF16)
    wr_hi = wr_hi_ref[...]
    logits = _dot(hi, wr_hi) + (_dot(lo, wr_hi) + _dot(hi, wr_lo_ref[...]))
    gates_ref[...] = _route(logits + br_ref[...])


def _out_proj(merged, x, w_out_b, layer, mod3, n_prompt_tiles, norm_g, w_router, b_router):
    n, d = x.shape
    w_router_hi = w_router.astype(BF16)
    row = pl.BlockSpec((ROW_TILE, d), lambda i: (i, 0))
    small = pl.BlockSpec((ROW_TILE, ROUTER_LANES), lambda i: (i, 0))
    return pl.pallas_call(
        _out_kernel, grid=(n // ROW_TILE,),
        in_specs=[row, row,
                  pl.BlockSpec((None, d, d), lambda i: (layer, 0, 0)),
                  _mod_spec(n_prompt_tiles, d, 2),
                  pl.BlockSpec((1, d), lambda i: (0, 0)),
                  _mod_spec(n_prompt_tiles, d, 4),
                  _mod_spec(n_prompt_tiles, d, 3),
                  pl.BlockSpec((d, ROUTER_LANES), lambda i: (0, 0)),
                  pl.BlockSpec((d, ROUTER_LANES), lambda i: (0, 0)),
                  pl.BlockSpec((1, ROUTER_LANES), lambda i: (0, 0))],
        out_specs=[row, row, small],
        out_shape=[jax.ShapeDtypeStruct((n, d), F32), jax.ShapeDtypeStruct((n, d), F32),
                   jax.ShapeDtypeStruct((n, ROUTER_LANES), F32)],
        compiler_params=_params("arbitrary"), name="out_proj",
    )(merged, x, w_out_b, mod3, norm_g.reshape(1, d), mod3, mod3, w_router_hi,
      (w_router - w_router_hi.astype(F32)).astype(BF16), b_router)


def _moe_kernel(tg_ref, src_ref, valid_ref, nact_ref, gates_ref, wg_ref, wu_ref, wd_ref, h_hbm, o_hbm,
                xbuf, ybuf, gsem, ssem):
    t = pl.program_id(0)
    tm = xbuf.shape[1]
    slot = t & 1
    n_active = nact_ref[0]

    def gather(tile, r):
        return pltpu.make_async_copy(h_hbm.at[pl.ds(src_ref[tile * tm + r], 1), :],
                                     xbuf.at[tile & 1, pl.ds(r, 1), :], gsem.at[tile & 1])

    def scatter(tile, r):
        return pltpu.make_async_copy(ybuf.at[tile & 1, pl.ds(r, 1), :],
                                     o_hbm.at[pl.ds(src_ref[tile * tm + r], 1), :], ssem.at[tile & 1])

    def each_row(fn):
        def body(r, c):
            fn(r)
            return c
        lax.fori_loop(0, tm, body, 0)

    def each_valid_row(tile, fn):
        def body(r, c):
            @pl.when(valid_ref[tile * tm + r] != 0)
            def _():
                fn(r)
            return c
        lax.fori_loop(0, tm, body, 0)

    @pl.when(t == 0)
    def _():
        each_row(lambda r: gather(0, r).start())

    @pl.when(t + 1 < n_active)
    def _():
        each_row(lambda r: gather(t + 1, r).start())

    @pl.when(t < n_active)
    def _():
        each_row(lambda r: gather(t, r).wait())

        @pl.when(t >= 2)
        def _():
            each_valid_row(t - 2, lambda r: scatter(t - 2, r).wait())

        x = xbuf[slot].astype(BF16)
        gates = gates_ref[...]
        lane = lax.broadcasted_iota(jnp.int32, gates.shape, 1)
        first = tg_ref[t] * EXPERTS_PER_GROUP
        acc = None
        for j in range(EXPERTS_PER_GROUP):
            gate = jnp.sum(jnp.where(lane == first + j, gates, 0.0), axis=-1, keepdims=True)
            hg = _dot(x, wg_ref[j])
            hu = _dot(x, wu_ref[j])
            act = (hg * jax.nn.sigmoid(hg)) * hu * gate
            contrib = _dot(act.astype(BF16), wd_ref[j])
            acc = contrib if acc is None else acc + contrib
        ybuf[slot] = acc
        each_valid_row(t, lambda r: scatter(t, r).start())

    @pl.when(t == pl.num_programs(0) - 1)
    def _():
        last = n_active - 1
        each_valid_row(last, lambda r: scatter(last, r).wait())

        @pl.when(last >= 1)
        def _():
            each_valid_row(last - 1, lambda r: scatter(last - 1, r).wait())


def _moe(h2, route, wg_b, wu_b, wd_b, layer):
    n, d = h2.shape
    f = wg_b.shape[3]
    tm = MOE_TILE
    n_tiles = n // tm + MOE_GROUPS
    n_sorted = n_tiles * tm
    grp = route[:, GROUP_LANE].astype(jnp.int32)
    onehot = grp[:, None] == jnp.arange(MOE_GROUPS, dtype=jnp.int32)[None, :]
    rank = jnp.sum(jnp.where(onehot, jnp.cumsum(onehot.astype(jnp.int32), axis=0) - 1, 0), axis=1)
    tiles_per = (jnp.sum(onehot.astype(jnp.int32), axis=0) + tm - 1) // tm
    tile_end = jnp.cumsum(tiles_per)
    dest = (tile_end - tiles_per)[grp] * tm + rank
    src = jnp.zeros((n_sorted,), jnp.int32).at[dest].set(jnp.arange(n, dtype=jnp.int32))
    valid = jnp.zeros((n_sorted,), jnp.int32).at[dest].set(1)
    tile_group = jnp.minimum(jnp.sum(jnp.arange(n_tiles, dtype=jnp.int32)[:, None] >= tile_end[None, :], axis=1),
                             MOE_GROUPS - 1).astype(jnp.int32)
    gates_sorted = route[src]

    def wspec(r, c):
        return pl.BlockSpec((None, EXPERTS_PER_GROUP, r, c), lambda t, tg, *_: (layer, tg[t], 0, 0))

    return pl.pallas_call(
        _moe_kernel,
        grid_spec=pltpu.PrefetchScalarGridSpec(
            num_scalar_prefetch=4,
            grid=(n_tiles,),
            in_specs=[pl.BlockSpec((tm, ROUTER_LANES), lambda t, *_: (t, 0)),
                      wspec(d, f), wspec(d, f), wspec(f, d),
                      pl.BlockSpec(memory_space=pl.ANY)],
            out_specs=pl.BlockSpec(memory_space=pl.ANY),
            scratch_shapes=[pltpu.VMEM((2, tm, d), F32), pltpu.VMEM((2, tm, d), F32),
                            pltpu.SemaphoreType.DMA((2,)), pltpu.SemaphoreType.DMA((2,))]),
        out_shape=jax.ShapeDtypeStruct((n, d), F32),
        compiler_params=_params("arbitrary"), name="moe",
    )(tile_group, src, valid, tile_end[-1:], gates_sorted, wg_b, wu_b, wd_b, h2)


def _residual_kernel(x_ref, f_ref, g2_ref, op_ref, os_ref, *, n_prompt_tiles):
    f3, g23 = _rows_mod(f_ref[...], g2_ref[...])
    x = x_ref[...] + (f3 * g23).reshape(x_ref.shape)

    def store(ref):
        ref[...] = x

    _store_split(n_prompt_tiles, op_ref, os_ref, store)


def _residual(x1, ffn, mod3, n_prompt_tiles):
    n, d = x1.shape
    seq = n_prompt_tiles * ROW_TILE
    row = pl.BlockSpec((ROW_TILE, d), lambda i: (i, 0))
    return pl.pallas_call(
        functools.partial(_residual_kernel, n_prompt_tiles=n_prompt_tiles), grid=(n // ROW_TILE,),
        in_specs=[row, row, _mod_spec(n_prompt_tiles, d, 5)],
        out_specs=_split_specs(n_prompt_tiles, (ROW_TILE, d)),
        out_shape=[jax.ShapeDtypeStruct((seq, d), F32), jax.ShapeDtypeStruct((n - seq, d), F32)],
        compiler_params=_params("arbitrary"), name="residual",
    )(x1, ffn, mod3)


def kernel(x_prompt, x_sample, c_prompt, c_sample, cache_k, cache_v, state_ssm_re, state_ssm_im, page_table, w_ada, b_ada, norm_mix_g, w_in, ssm_a_re, ssm_a_im, ssm_log_dt, ssm_b_re, ssm_b_im, ssm_c_re, ssm_c_im, ssm_d, w_glu, b_glu, q_norm_g, k_norm_g, sb_bias, w_branch_ssm, w_branch_attn, w_out, norm_ffn_g, w_router_group, b_router_group, w_router_expert, b_router_expert, w_exp_gate, w_exp_up, w_exp_down):
    n_layers = w_ada.shape[0]
    bsz, seq, d = x_prompt.shape
    n_seq, n_q, _ = x_sample.shape
    assert bsz == 1 and n_q == SUBLANES and n_seq * n_q == ROW_TILE and seq % ROW_TILE == 0
    n_s = n_seq * n_q
    n_prompt_tiles = seq // ROW_TILE
    mod_rows = ROW_TILE // SUBLANES
    ssm_w = ssm_d.shape[1]
    attn_w = w_branch_attn.shape[1]
    assert ssm_w == attn_w
    n_heads = attn_w // HEAD_DIM
    n_groups = ssm_w // SSM_GROUP
    page = cache_k.shape[2]
    chunk_p, chunk_s = 16, n_q
    assert seq % chunk_p == 0 and chunk_p == 2 * chunk_s and page == HEAD_DIM
    chunks = (chunk_p, chunk_s)

    w_in_b = w_in.astype(BF16)
    w_glu_b = w_glu.astype(BF16)
    w_bs_b = w_branch_ssm.astype(BF16)
    w_ba_b = w_branch_attn.astype(BF16)
    w_out_b = w_out.astype(BF16)
    wg_b, wu_b, wd_b = w_exp_gate.astype(BF16), w_exp_up.astype(BF16), w_exp_down.astype(BF16)

    c_all = jnp.concatenate([c_prompt, c_sample], axis=0)
    pad = -c_all.shape[0] % SUBLANES
    mod = _adaln(jnp.pad(c_all, ((0, pad), (0, 0))), w_ada, b_ada)

    x = jnp.concatenate([x_prompt.reshape(seq, d), x_sample.reshape(n_s, d)], axis=0)
    outs = {k: [] for k in ("rp", "ip", "rs", "is")}
    kv_out = tuple(jnp.zeros((n_layers, rows * n_heads, HEAD_DIM), F32) for rows in (seq, n_s, seq, n_s))
    for l in range(n_layers):
        mod3_prev = mod3 if l else None
        mod3 = jnp.stack([jnp.broadcast_to(mod[l, 0], (mod_rows, N_MOD * d)), mod[l, 1:1 + n_seq]])
        if l == 0:
            h = _norm_mod(x, norm_mix_g[l], mod3, n_prompt_tiles, 1, 0)
        else:
            x, h = _res_norm_mod(x1, ffn, mod3_prev, norm_mix_g[l], mod3, n_prompt_tiles)
        u, up, us, q_b, k_b, v_b, gates, kv_out = _in_proj(h, w_in_b, l, q_norm_g[l], k_norm_g[l], ssm_w,
                                                           n_prompt_tiles, kv_out, chunks)

        ssm_par = (ssm_a_re[l], ssm_a_im[l], ssm_log_dt[l], ssm_b_re[l], ssm_b_im[l], ssm_c_re[l], ssm_c_im[l])
        prep_p = _ssm_prep(*ssm_par, chunk_p)
        loc_re, loc_im = _ssm_loc(up, prep_p[1], prep_p[2])
        sp_re, sp_im = _ssm_scan(loc_re, loc_im, prep_p[5], prep_p[6])
        yp, np_re, np_im = _ssm_out(up, sp_re, sp_im, prep_p, False)
        h0_re = state_ssm_re[l].reshape(n_seq, n_groups * SSM_STATE)
        h0_im = state_ssm_im[l].reshape(n_seq, n_groups * SSM_STATE)
        ys, ns_re, ns_im = _ssm_out(us, h0_re, h0_im, prep_p, True)
        a_branch = _glu(yp, ys, u, ssm_d[l], w_glu_b, l, b_glu[l], n_prompt_tiles, chunks)

        bias2 = sb_bias[l].astype(F32) * LOG2_E
        bp = _attn_prompt(q_b, k_b, v_b, bias2, seq)
        new_pad = ((0, 0), (0, page - n_q), (0, 0))
        bs = _attn_sample(q_b[seq:].reshape(n_seq, n_q, attn_w),
                          jnp.pad(k_b[seq:].reshape(n_seq, n_q, attn_w), new_pad),
                          jnp.pad(v_b[seq:].reshape(n_seq, n_q, attn_w), new_pad),
                          cache_k, cache_v, l, page_table, bias2)
        b_branch = jnp.concatenate([bp, bs.reshape(n_s, attn_w).astype(BF16)], axis=0)

        merged = _merge(a_branch, b_branch, gates, w_bs_b, w_ba_b, l)
        w_router = jnp.pad(jnp.concatenate([w_router_expert[l], w_router_group[l]], axis=1),
                           ((0, 0), (0, ROUTER_LANES - N_EXPERTS - MOE_GROUPS)))
        b_router = jnp.pad(jnp.concatenate([b_router_expert[l], b_router_group[l]]),
                           (0, ROUTER_LANES - N_EXPERTS - MOE_GROUPS)).reshape(1, ROUTER_LANES)
        x1, h2, route = _out_proj(merged, x, w_out_b, l, mod3, n_prompt_tiles, norm_ffn_g[l], w_router, b_router)
        ffn = _moe(h2, route, wg_b, wu_b, wd_b, l)

        outs["rp"].append(np_re[-1].reshape(1, n_groups, SSM_STATE))
        outs["ip"].append(np_im[-1].reshape(1, n_groups, SSM_STATE))
        outs["rs"].append(ns_re.reshape(n_seq, n_groups, SSM_STATE))
        outs["is"].append(ns_im.reshape(n_seq, n_groups, SSM_STATE))

    y_p, y_s = _residual(x1, ffn, mod3, n_prompt_tiles)
    st = lambda k: jnp.stack(outs[k])
    kp, ks, vp, vs = kv_out
    return (y_p.reshape(1, seq, d), y_s.reshape(n_seq, n_q, d),
            kp.reshape(n_layers, 1, seq, n_heads, HEAD_DIM), vp.reshape(n_layers, 1, seq, n_heads, HEAD_DIM),
            st("rp"), st("ip"),
            ks.reshape(n_layers, n_seq, n_q, n_heads, HEAD_DIM), vs.reshape(n_layers, n_seq, n_q, n_heads, HEAD_DIM),
            st("rs"), st("is"))
```

```python
import functools
import math

import jax
import jax.numpy as jnp
from jax import lax
from jax.experimental import pallas as pl
from jax.experimental.pallas import tpu as pltpu

F32 = jnp.float32
BF16 = jnp.bfloat16

RMS_EPS = 1e-6
HEAD_DIM = 128
SSM_GROUP = 16
SSM_STATE = 64
PAIR_CH = 2 * SSM_GROUP
PAIR_ST = 2 * SSM_STATE
LANES = 128
PAIRS_PER_SLAB = LANES // PAIR_CH
MOE_GROUPS = 4
EXPERTS_PER_GROUP = 4
N_EXPERTS = MOE_GROUPS * EXPERTS_PER_GROUP
ROUTER_LANES = 128
N_MOD = 6
ROW_TILE = 256
SUBLANES = 8
VMEM_LIMIT_BYTES = 56 * 1024 * 1024
ATTN_TQ = 256
ATTN_TK = 256
MOE_TILE = 256
GROUP_LANE = N_EXPERTS
PROMPT_TILES_PER_BODY = 8
LOG2_E = 1.4426950408889634
Q_SCALE_LOG2 = HEAD_DIM ** -0.5 * LOG2_E


def _params(*sem):
    return pltpu.CompilerParams(dimension_semantics=sem, vmem_limit_bytes=VMEM_LIMIT_BYTES)


def _dot(a, b):
    return jnp.dot(a, b, preferred_element_type=F32)


def _dot_nt(a, b):
    return lax.dot_general(a, b, (((1,), (1,)), ((), ())), preferred_element_type=F32)


def _dot_exact(a, b):
    return jnp.dot(a, b, preferred_element_type=F32, precision=lax.Precision.HIGHEST)


def _adaln_kernel(c_ref, w_ref, b_ref, o_ref):
    c = c_ref[...]
    a = (c * jax.nn.sigmoid(c)).astype(BF16)
    o_ref[...] = _dot(a, w_ref[...].astype(BF16)) + b_ref[...]


def _adaln(c_all, w_ada, b_ada):
    n_layers, d, n_out = w_ada.shape
    r = c_all.shape[0]
    tn = 1024
    return pl.pallas_call(
        _adaln_kernel,
        grid=(n_layers, n_out // tn),
        in_specs=[pl.BlockSpec((r, d), lambda l, j: (0, 0)),
                  pl.BlockSpec((None, d, tn), lambda l, j: (l, 0, j)),
                  pl.BlockSpec((None, 1, tn), lambda l, j: (l, 0, j))],
        out_specs=pl.BlockSpec((None, r, tn), lambda l, j: (l, 0, j)),
        out_shape=jax.ShapeDtypeStruct((n_layers, r, n_out), F32),
        compiler_params=_params("arbitrary", "arbitrary"),
        name="adaln",
    )(c_all, w_ada, b_ada.reshape(n_layers, 1, n_out))


def _rows_mod(v, m):
    tm, d = v.shape
    return v.reshape(tm // SUBLANES, SUBLANES, d), m[:, None, :]


def _modulated_norm(x, g, sc, sh):
    ms = jnp.mean(x * x, axis=-1, keepdims=True)
    y = x * lax.rsqrt(ms + RMS_EPS) * g
    y3, sc3 = _rows_mod(y, sc)
    h = y3 * (1.0 + sc3) + sh[:, None, :]
    return h.reshape(x.shape)


def _mod_spec(n_prompt_tiles, d, which):
    return pl.BlockSpec((None, ROW_TILE // SUBLANES, d),
                        lambda i: (jnp.where(i < n_prompt_tiles, 0, 1), 0, which))


def _norm_mod_kernel(x_ref, g_ref, sc_ref, sh_ref, o_ref):
    o_ref[...] = _modulated_norm(x_ref[...], g_ref[...], sc_ref[...], sh_ref[...]).astype(BF16)


def _res_norm_mod_kernel(x1_ref, f_ref, g2_ref, g_ref, sc_ref, sh_ref, x_ref, h_ref):
    f3, g23 = _rows_mod(f_ref[...], g2_ref[...])
    x = x1_ref[...] + (f3 * g23).reshape(x_ref.shape)
    x_ref[...] = x
    h_ref[...] = _modulated_norm(x, g_ref[...], sc_ref[...], sh_ref[...]).astype(BF16)


def _res_norm_mod(x1, ffn, mod3_prev, g, mod3, n_prompt_tiles):
    n, d = x1.shape
    row = pl.BlockSpec((ROW_TILE, d), lambda i: (i, 0))
    return pl.pallas_call(
        _res_norm_mod_kernel,
        grid=(n // ROW_TILE,),
        in_specs=[row, row, _mod_spec(n_prompt_tiles, d, 5),
                  pl.BlockSpec((1, d), lambda i: (0, 0)),
                  _mod_spec(n_prompt_tiles, d, 1), _mod_spec(n_prompt_tiles, d, 0)],
        out_specs=[row, row],
        out_shape=[jax.ShapeDtypeStruct((n, d), F32), jax.ShapeDtypeStruct((n, d), BF16)],
        compiler_params=_params("arbitrary"),
        name="res_norm_mod",
    )(x1, ffn, mod3_prev, g.reshape(1, d), mod3, mod3)


def _norm_mod(x, g, mod3, n_prompt_tiles, sc_idx, sh_idx):
    n, d = x.shape
    return pl.pallas_call(
        _norm_mod_kernel,
        grid=(n // ROW_TILE,),
        in_specs=[pl.BlockSpec((ROW_TILE, d), lambda i: (i, 0)),
                  pl.BlockSpec((1, d), lambda i: (0, 0)),
                  _mod_spec(n_prompt_tiles, d, sc_idx),
                  _mod_spec(n_prompt_tiles, d, sh_idx)],
        out_specs=pl.BlockSpec((ROW_TILE, d), lambda i: (i, 0)),
        out_shape=jax.ShapeDtypeStruct((n, d), BF16),
        compiler_params=_params("arbitrary"),
        name="norm_mod",
    )(x, g.reshape(1, d), mod3, mod3)


def _store_token_major(ref, val):
    tm = val.shape[0]
    n_heads = val.shape[1] // HEAD_DIM
    for hh in range(n_heads):
        ref[pl.ds(hh, tm, stride=n_heads), :] = val[:, hh * HEAD_DIM:(hh + 1) * HEAD_DIM]


def _split_specs(n_prompt_tiles, block, lead=()):
    zeros = (0,) * (len(block) - 1)
    nones = (None,) * len(lead)
    return [pl.BlockSpec(nones + block, lambda i: lead + (jnp.minimum(i, n_prompt_tiles - 1),) + zeros),
            pl.BlockSpec(nones + block, lambda i: lead + (0,) + zeros)]


def _store_split(n_prompt_tiles, prompt_ref, sample_ref, store):
    i = pl.program_id(0)

    @pl.when(i < n_prompt_tiles)
    def _():
        store(prompt_ref)

    @pl.when(i >= n_prompt_tiles)
    def _():
        store(sample_ref)


def _proj_uv_kernel(h_ref, wu_ref, wv_ref, *rest, n_prompt_tiles, chunks):
    u_ref, up_ref, us_ref, vb_ref, vp_ref, vs_ref, slab_ref = rest[-7:]
    h = h_ref[...]
    u = _dot(h, wu_ref[...])
    u_ref[...] = u
    _store_split(n_prompt_tiles, up_ref, us_ref,
                 lambda ref: _rows_to_pairs(slab_ref, u, ref, chunks[0] if ref is up_ref else chunks[1]))
    v = _dot(h, wv_ref[...])
    vb_ref[...] = v.astype(BF16)
    _store_split(n_prompt_tiles, vp_ref, vs_ref, lambda ref: _store_token_major(ref, v))


def _head_norm(acc, g):
    outs = []
    for hh in range(acc.shape[1] // HEAD_DIM):
        blk = acc[:, hh * HEAD_DIM:(hh + 1) * HEAD_DIM]
        ms = jnp.mean(blk * blk, axis=-1, keepdims=True)
        outs.append(blk * lax.rsqrt(ms + RMS_EPS) * g)
    return jnp.concatenate(outs, axis=1)


def _proj_qk_kernel(h_ref, wq_ref, wk_ref, gq_ref, gk_ref, *rest, n_prompt_tiles):
    qb_ref, kb_ref, kp_ref, ks_ref = rest[-4:]
    h = h_ref[...]
    qb_ref[...] = (_head_norm(_dot(h, wq_ref[...]), gq_ref[...]) * Q_SCALE_LOG2).astype(BF16)
    k = _head_norm(_dot(h, wk_ref[...]), gk_ref[...])
    kb_ref[...] = k.astype(BF16)
    _store_split(n_prompt_tiles, kp_ref, ks_ref, lambda ref: _store_token_major(ref, k))


def _proj_gate_kernel(h_ref, w_ref, o_ref):
    o_ref[...] = jax.nn.sigmoid(_dot(h_ref[...], w_ref[...])).astype(BF16)


def _in_proj(h, w_in_b, layer, q_g, k_g, width, n_prompt_tiles, kv_prev, chunks):
    n, d = h.shape
    nt = n // ROW_TILE
    n_layers = w_in_b.shape[0]
    row = pl.BlockSpec((ROW_TILE, d), lambda i: (i, 0))
    out = pl.BlockSpec((ROW_TILE, width), lambda i: (i, 0))

    def wcol(j):
        return pl.BlockSpec((None, d, width), lambda i: (layer, 0, j))

    sds = lambda dt: jax.ShapeDtypeStruct((n, width), dt)
    n_heads = width // HEAD_DIM
    tok_rows = ROW_TILE * n_heads
    tok = _split_specs(n_prompt_tiles, (tok_rows, HEAD_DIM), (layer,))
    tok_sds = [jax.ShapeDtypeStruct((n_layers, n_prompt_tiles * tok_rows, HEAD_DIM), F32),
               jax.ShapeDtypeStruct((n_layers, (nt - n_prompt_tiles) * tok_rows, HEAD_DIM), F32)]
    assert nt - n_prompt_tiles == 1
    prev_k, prev_v = kv_prev[:2], kv_prev[2:]
    prev_specs = [pl.BlockSpec(memory_space=pl.ANY)] * 2
    n_prev = len(prev_specs)
    n_pairs = width // PAIR_CH
    pair_sds = [jax.ShapeDtypeStruct((n_pairs, n_prompt_tiles * ROW_TILE // chunks[0], chunks[0] * PAIR_CH), BF16),
                jax.ShapeDtypeStruct((n_pairs, ROW_TILE // chunks[1], chunks[1] * PAIR_CH), BF16)]
    u, u_p, u_s, v_b, v_p, v_s = pl.pallas_call(
        functools.partial(_proj_uv_kernel, n_prompt_tiles=n_prompt_tiles, chunks=chunks), grid=(nt,),
        in_specs=[row, wcol(0), wcol(3)] + prev_specs,
        out_specs=[out] + _pair_specs(n_prompt_tiles, n_pairs, chunks) + [out] + tok,
        out_shape=[sds(F32)] + pair_sds + [sds(BF16)] + tok_sds,
        input_output_aliases={3 + j: 4 + j for j in range(n_prev)},
        scratch_shapes=[pltpu.VMEM((width // LANES, ROW_TILE, LANES), F32)],
        compiler_params=_params("arbitrary"), name="proj_uv",
    )(h, w_in_b, w_in_b, *prev_v)
    gspec = pl.BlockSpec((1, HEAD_DIM), lambda i: (0, 0))
    q_b, k_b, k_p, k_s = pl.pallas_call(
        functools.partial(_proj_qk_kernel, n_prompt_tiles=n_prompt_tiles), grid=(nt,),
        in_specs=[row, wcol(1), wcol(2), gspec, gspec] + prev_specs, out_specs=[out, out] + tok,
        out_shape=[sds(BF16), sds(BF16)] + tok_sds,
        input_output_aliases={5 + j: 2 + j for j in range(n_prev)},
        compiler_params=_params("arbitrary"), name="proj_qk",
    )(h, w_in_b, w_in_b, q_g.reshape(1, HEAD_DIM), k_g.reshape(1, HEAD_DIM), *prev_k)
    n_gate = (w_in_b.shape[2] - 4 * width) // width
    gates = pl.pallas_call(
        _proj_gate_kernel, grid=(n_gate, nt),
        in_specs=[pl.BlockSpec((ROW_TILE, d), lambda j, i: (i, 0)),
                  pl.BlockSpec((None, d, width), lambda j, i: (layer, 0, 4 + j))],
        out_specs=pl.BlockSpec((ROW_TILE, width), lambda j, i: (i, j)),
        out_shape=jax.ShapeDtypeStruct((n, n_gate * width), BF16),
        compiler_params=_params("arbitrary", "arbitrary"), name="proj_gate",
    )(h, w_in_b)
    return u, u_p, u_s, q_b, k_b, v_b, gates, (k_p, k_s, v_p, v_s)


def _cmul(ar, ai, br, bi):
    return ar * br - ai * bi, ar * bi + ai * br


def _discretise(a_re, a_im, log_dt):
    dt = jnp.exp(log_dt)
    mag = jnp.exp(dt * a_re)
    ab_re = mag * jnp.cos(dt * a_im)
    ab_im = mag * jnp.sin(dt * a_im)
    den = a_re * a_re + a_im * a_im
    f_re = ((ab_re - 1.0) * a_re + ab_im * a_im) / den
    f_im = (ab_im * a_re - (ab_re - 1.0) * a_im) / den
    return dt, f_re, f_im


def _abar_pow(a_re, a_im, dt, k):
    mag = jnp.exp(k * dt * a_re)
    th = k * dt * a_im
    return mag * jnp.cos(th), mag * jnp.sin(th)


def _ssm_prep_kernel(as_re_ref, as_im_ref, ldts_ref, al_re_ref, al_im_ref, ldtl_ref,
                     bt_re_ref, bt_im_ref, btr_re_ref, btr_im_ref, c_re_ref, c_im_ref,
                     ct_re_ref, ct_im_ref,
                     m_ref, e_re_ref, e_im_ref, f_re_ref, f_im_ref, p_re_ref, p_im_ref, *, chunk):
    width = chunk * PAIR_CH
    as_re, as_im = as_re_ref[...], as_im_ref[...]
    dts, fs_re, fs_im = _discretise(as_re, as_im, ldts_ref[...])
    al_re, al_im = al_re_ref[...], al_im_ref[...]
    dtl, fl_re, fl_im = _discretise(al_re, al_im, ldtl_ref[...])

    lane_blk = lax.shift_right_logical(lax.broadcasted_iota(jnp.int32, (1, width), 1), 5)
    assert PAIR_CH == 32

    k_lane = lax.broadcasted_iota(jnp.int32, (1, LANES), 1)
    assert chunk < LANES
    pw_re, pw_im = _abar_pow(as_re, as_im, dts, jnp.minimum(k_lane, chunk).astype(F32))
    k_row = lax.broadcasted_iota(jnp.int32, (LANES, width), 0)

    def spread(lane_to_k):
        onehot = (k_row == lane_to_k).astype(F32)
        return _dot_exact(pw_re, onehot), _dot_exact(pw_im, onehot)

    pk_re, pk_im = spread(chunk - 1 - lane_blk)
    bb_re, bb_im = _cmul(fs_re, fs_im, bt_re_ref[...], bt_im_ref[...])
    w_re, w_im = _cmul(pk_re, pk_im, bb_re, bb_im)
    kr = _dot_exact(c_re_ref[...], w_re) - _dot_exact(c_im_ref[...], w_im)
    lane = lax.broadcasted_iota(jnp.int32, (PAIR_CH, width), 1)
    for t in range(chunk):
        shift = (chunk - 1 - t) * PAIR_CH
        rolled = kr if shift == 0 else pltpu.roll(kr, width - shift, axis=1)
        m_ref[t * PAIR_CH:(t + 1) * PAIR_CH, :] = jnp.where(lane < (t + 1) * PAIR_CH, rolled, 0.0)

    bbt_re, bbt_im = _cmul(fl_re, fl_im, btr_re_ref[...], btr_im_ref[...])
    for tau in range(chunk):
        q_re, q_im = _abar_pow(al_re, al_im, dtl, float(chunk - 1 - tau))
        e_re, e_im = _cmul(q_re, q_im, bbt_re, bbt_im)
        e_re_ref[tau * PAIR_CH:(tau + 1) * PAIR_CH, :] = e_re
        e_im_ref[tau * PAIR_CH:(tau + 1) * PAIR_CH, :] = e_im

    pf_re, pf_im = spread(lane_blk + 1)
    ca_re, ca_im = _cmul(ct_re_ref[...], ct_im_ref[...], pf_re, pf_im)
    f_re_ref[...] = ca_re
    f_im_ref[...] = -ca_im

    pt_re, pt_im = _abar_pow(al_re, al_im, dtl, float(chunk))
    ph_re, ph_im = _abar_pow(al_re, al_im, dtl, float(chunk // 2))
    p_re_ref[...] = jnp.concatenate([pt_re, ph_re], axis=0)
    p_im_ref[...] = jnp.concatenate([pt_im, ph_im], axis=0)


def _ssm_layouts(a_re, a_im, log_dt, b_re, b_im, c_re, c_im, chunk):
    n_groups = a_re.shape[0]
    n_pairs = n_groups // 2
    eye = jnp.eye(2, dtype=F32)

    def sub(x):
        return x.reshape(n_pairs, PAIR_ST, 1)

    def lan(x):
        return x.reshape(n_pairs, 1, PAIR_ST)

    ldt = jnp.broadcast_to(log_dt[:, None], (n_groups, SSM_STATE))

    def b_tiled(b):
        bp = b.reshape(n_pairs, 2, SSM_STATE, SSM_GROUP)
        bd = jnp.einsum('ngph,gk->ngpkh', bp, eye).reshape(n_pairs, PAIR_ST, PAIR_CH)
        return jnp.tile(bd, (1, 1, chunk))

    def b_rows(b):
        bp = b.reshape(n_pairs, 2, SSM_STATE, SSM_GROUP)
        return jnp.einsum('ngph,gk->nghkp', bp, eye).reshape(n_pairs, PAIR_CH, PAIR_ST)

    def c_rows(c):
        cp = c.reshape(n_pairs, 2, SSM_GROUP, SSM_STATE)
        return jnp.einsum('nghp,gk->nghkp', cp, eye).reshape(n_pairs, PAIR_CH, PAIR_ST)

    def c_tiled(c):
        cp = c.reshape(n_pairs, 2, SSM_GROUP, SSM_STATE)
        cd = jnp.einsum('nghp,gk->nkpgh', cp, eye).reshape(n_pairs, PAIR_ST, PAIR_CH)
        return jnp.tile(cd, (1, 1, chunk))

    return (sub(a_re), sub(a_im), sub(ldt), lan(a_re), lan(a_im), lan(ldt),
            b_tiled(b_re), b_tiled(b_im), b_rows(b_re), b_rows(b_im),
            c_rows(c_re), c_rows(c_im), c_tiled(c_re), c_tiled(c_im))


def _ssm_prep(a_re, a_im, log_dt, b_re, b_im, c_re, c_im, chunk):
    n_pairs = a_re.shape[0] // 2
    width = chunk * PAIR_CH
    args = _ssm_layouts(a_re, a_im, log_dt, b_re, b_im, c_re, c_im, chunk)

    def spec(r, c):
        return pl.BlockSpec((None, r, c), lambda i: (i, 0, 0))

    sub, lan = spec(PAIR_ST, 1), spec(1, PAIR_ST)
    wide, rows = spec(PAIR_ST, width), spec(PAIR_CH, PAIR_ST)
    sds = lambda r, c: jax.ShapeDtypeStruct((n_pairs, r, c), F32)
    return pl.pallas_call(
        functools.partial(_ssm_prep_kernel, chunk=chunk),
        grid=(n_pairs,),
        in_specs=[sub, sub, sub, lan, lan, lan, wide, wide, rows, rows, rows, rows, wide, wide],
        out_specs=[spec(width, width), spec(width, PAIR_ST), spec(width, PAIR_ST),
                   wide, wide, spec(2, PAIR_ST), spec(2, PAIR_ST)],
        out_shape=[sds(width, width), sds(width, PAIR_ST), sds(width, PAIR_ST),
                   sds(PAIR_ST, width), sds(PAIR_ST, width), sds(2, PAIR_ST), sds(2, PAIR_ST)],
        compiler_params=_params("arbitrary"),
        name="ssm_prep",
    )(*args)


def _ssm_loc_kernel(u_ref, e_re_ref, e_im_ref, l_re_ref, l_im_ref):
    ub = u_ref[...]
    l_re_ref[...] = _dot(ub, e_re_ref[...].astype(BF16))
    l_im_ref[...] = _dot(ub, e_im_ref[...].astype(BF16))


def _ssm_loc(u_pairs, e_re, e_im):
    n_pairs, rows, width = u_pairs.shape
    st = pl.BlockSpec((rows, PAIR_ST), lambda i: (0, i))
    sds = jax.ShapeDtypeStruct((rows, n_pairs * PAIR_ST), F32)
    espec = pl.BlockSpec((None, width, PAIR_ST), lambda i: (i, 0, 0))
    return pl.pallas_call(
        _ssm_loc_kernel, grid=(n_pairs,),
        in_specs=[pl.BlockSpec((None, rows, width), lambda i: (i, 0, 0)), espec, espec],
        out_specs=[st, st], out_shape=[sds, sds],
        compiler_params=_params("arbitrary"), name="ssm_loc",
    )(u_pairs, e_re, e_im)


def _ssm_scan_kernel(l_re_ref, l_im_ref, p_re_ref, p_im_ref, s_re_ref, s_im_ref):
    p_re, p_im = p_re_ref[...], p_im_ref[...]
    n_chunks = l_re_ref.shape[0]

    def body(c, s):
        s_re, s_im = s
        s_re_ref[pl.ds(c, 1), :] = s_re
        s_im_ref[pl.ds(c, 1), :] = s_im
        n_re, n_im = _cmul(p_re, p_im, s_re, s_im)
        return n_re + l_re_ref[pl.ds(c, 1), :], n_im + l_im_ref[pl.ds(c, 1), :]

    zero = jnp.zeros_like(p_re)
    lax.fori_loop(0, n_chunks, body, (zero, zero))


def _ssm_scan(loc_re, loc_im, p_re, p_im):
    rows, n_state = loc_re.shape
    tc = 1024
    blk = pl.BlockSpec((rows, tc), lambda j: (0, j))
    par = pl.BlockSpec((1, tc), lambda j: (0, j))
    sds = jax.ShapeDtypeStruct((rows, n_state), F32)
    return pl.pallas_call(
        _ssm_scan_kernel, grid=(n_state // tc,),
        in_specs=[blk, blk, par, par], out_specs=[blk, blk], out_shape=[sds, sds],
        compiler_params=_params("arbitrary"), name="ssm_scan",
    )(loc_re, loc_im, p_re[:, 0].reshape(1, n_state), p_im[:, 0].reshape(1, n_state))


def _gelu_tanh(x):
    c = math.sqrt(2.0 / math.pi)
    return 0.5 * x * (1.0 + jnp.tanh(c * (x + 0.044715 * (x * x * x))))


def _ssm_out_kernel(u_ref, s_re_ref, s_im_ref, m_ref, e_re_ref, e_im_ref, f_re_ref, f_im_ref,
                    p_re_ref, p_im_ref, y_ref, n_re_ref, n_im_ref):
    ub = u_ref[...]
    s_re, s_im = s_re_ref[...], s_im_ref[...]
    y = (_dot_nt(ub, m_ref[...].astype(BF16))
         + _dot(s_re.astype(BF16), f_re_ref[...].astype(BF16))
         + _dot(s_im.astype(BF16), f_im_ref[...].astype(BF16)))
    y_ref[...] = y.astype(BF16)
    a_re, a_im = _cmul(p_re_ref[...], p_im_ref[...], s_re, s_im)
    n_re_ref[...] = a_re + _dot(ub, e_re_ref[...].astype(BF16))
    n_im_ref[...] = a_im + _dot(ub, e_im_ref[...].astype(BF16))


def _ssm_out(u_pairs, s_re, s_im, prep, half):
    n_pairs, rows, width = u_pairs.shape
    m, e_re, e_im, f_re, f_im, p_re, p_im = prep
    st = pl.BlockSpec((rows, PAIR_ST), lambda i: (0, i))
    per = lambda r, c: pl.BlockSpec((None, r, c), lambda i: (i, 0, 0))
    e_spec = pl.BlockSpec((None, width, PAIR_ST), lambda i: (i, 1 if half else 0, 0))
    p_spec = pl.BlockSpec((None, 1, PAIR_ST), lambda i: (i, 0, 0))
    sds = jax.ShapeDtypeStruct((rows, n_pairs * PAIR_ST), F32)
    pick = slice(1, 2) if half else slice(0, 1)
    return pl.pallas_call(
        _ssm_out_kernel, grid=(n_pairs,),
        in_specs=[per(rows, width), st, st, per(width, width), e_spec, e_spec,
                  per(PAIR_ST, width), per(PAIR_ST, width), p_spec, p_spec],
        out_specs=[per(rows, width), st, st],
        out_shape=[jax.ShapeDtypeStruct((n_pairs, rows, width), BF16), sds, sds],
        compiler_params=_params("arbitrary"), name="ssm_out",
    )(u_pairs, s_re, s_im, m, e_re, e_im, f_re, f_im, p_re[:, pick], p_im[:, pick])


def _rows_to_pairs(slab_ref, val, out_ref, chunk):
    tm, c = val.shape
    rows = tm // chunk
    for s in range(c // LANES):
        slab_ref[s] = val[:, s * LANES:(s + 1) * LANES]
    for step in range(chunk):
        dst = (step % PAIRS_PER_SLAB) * PAIR_CH
        tile0 = (step // PAIRS_PER_SLAB) * LANES
        for s in range(c // LANES):
            x = slab_ref[s, pl.ds(step, rows, stride=chunk), :]
            for j in range(PAIRS_PER_SLAB):
                shift = (dst - j * PAIR_CH) % LANES
                moved = x if shift == 0 else pltpu.roll(x, shift, axis=1)
                out_ref[s * PAIRS_PER_SLAB + j, :, tile0 + dst:tile0 + dst + PAIR_CH] = (
                    moved[:, dst:dst + PAIR_CH].astype(BF16))


def _pairs_to_rows(in_ref, slab_ref, chunk):
    n_pairs, rows, _ = in_ref.shape
    lane_pair = lax.shift_right_logical(lax.broadcasted_iota(jnp.int32, (rows, LANES), 1), 5)
    assert PAIR_CH == 32
    for step in range(chunk):
        src = (step % PAIRS_PER_SLAB) * PAIR_CH
        tile0 = (step // PAIRS_PER_SLAB) * LANES
        for s in range(n_pairs // PAIRS_PER_SLAB):
            acc = None
            for j in range(PAIRS_PER_SLAB):
                x = in_ref[s * PAIRS_PER_SLAB + j, :, tile0:tile0 + LANES].astype(F32)
                shift = (j * PAIR_CH - src) % LANES
                moved = x if shift == 0 else pltpu.roll(x, shift, axis=1)
                acc = moved if acc is None else jnp.where(lane_pair == j, moved, acc)
            slab_ref[s, pl.ds(step, rows, stride=chunk), :] = acc


def _glu_kernel(yp_ref, ys_ref, u_ref, d_ref, w_ref, b_ref, o_ref, slab_ref, *, n_prompt_tiles, chunks):
    i = pl.program_id(0)

    @pl.when(i < n_prompt_tiles)
    def _():
        _pairs_to_rows(yp_ref, slab_ref, chunks[0])

    @pl.when(i >= n_prompt_tiles)
    def _():
        _pairs_to_rows(ys_ref, slab_ref, chunks[1])

    y_ssm = jnp.concatenate([slab_ref[s] for s in range(slab_ref.shape[0])], axis=1)
    y = _gelu_tanh(y_ssm + d_ref[...] * u_ref[...])
    t = _dot(y.astype(BF16), w_ref[...]) + b_ref[...]
    o_ref[...] = (y * jax.nn.sigmoid(t)).astype(BF16)


def _pair_specs(n_prompt_tiles, n_pairs, chunks):
    return [pl.BlockSpec((n_pairs, ROW_TILE // chunks[0], chunks[0] * PAIR_CH),
                         lambda i: (0, jnp.minimum(i, n_prompt_tiles - 1), 0)),
            pl.BlockSpec((n_pairs, ROW_TILE // chunks[1], chunks[1] * PAIR_CH), lambda i: (0, 0, 0))]


def _glu(y_p, y_s, u, d_skip, w_b, layer, b, n_prompt_tiles, chunks):
    n, c = u.shape
    row = pl.BlockSpec((ROW_TILE, c), lambda i: (i, 0))
    vec = pl.BlockSpec((1, c), lambda i: (0, 0))
    return pl.pallas_call(
        functools.partial(_glu_kernel, n_prompt_tiles=n_prompt_tiles, chunks=chunks), grid=(n // ROW_TILE,),
        in_specs=_pair_specs(n_prompt_tiles, c // PAIR_CH, chunks)
        + [row, vec, pl.BlockSpec((None, c, c), lambda i: (layer, 0, 0)), vec],
        out_specs=row,
        out_shape=jax.ShapeDtypeStruct((n, c), BF16),
        scratch_shapes=[pltpu.VMEM((c // LANES, ROW_TILE, LANES), F32)],
        compiler_params=_params("arbitrary"), name="glu",
    )(y_p, y_s, u, d_skip.reshape(1, c), w_b, b.reshape(1, c))


def _suffix_matrix(tk):
    j = jnp.arange(tk)[:, None]
    s = jnp.arange(tk + HEAD_DIM)[None, :]
    return ((j > s) | (s >= tk)).astype(BF16)


def _stick_weights(z2, carry, tri, mask, split):
    tk = z2.shape[1]
    neg_abs = lax.bitcast_convert_type(
        lax.bitcast_convert_type(z2, jnp.uint32) | jnp.uint32(0x80000000), F32)
    sp = jnp.maximum(z2, 0.0) + jnp.log2(1.0 + jnp.exp2(neg_abs))
    spm = sp if mask is None else jnp.where(mask, sp, 0.0)
    hi = spm.astype(BF16)
    tot = _dot(hi, tri)
    if split:
        tot = tot + _dot((spm - hi.astype(F32)).astype(BF16), tri)
    w = jnp.exp2((z2 - sp) - tot[:, :tk] - jnp.concatenate([carry] * (tk // HEAD_DIM), axis=1))
    if mask is not None:
        w = jnp.where(mask, w, 0.0)
    return w, carry + tot[:, tk:]


def _attn_prompt_kernel(bias_ref, q_ref, k_ref, v_ref, tri_ref, o_ref, carry_ref, acc_ref):
    hh = pl.program_id(0)
    i = pl.program_id(1)
    tq, tk = ATTN_TQ, ATTN_TK
    bias = bias_ref[hh]
    q = q_ref[...]
    tri = tri_ref[...]

    def tile(j, carry, acc, mask):
        k = k_ref[pl.ds(pl.multiple_of(j * tk, tk), tk), :]
        v = v_ref[pl.ds(pl.multiple_of(j * tk, tk), tk), :]
        w, carry = _stick_weights(_dot_nt(q, k) + bias, carry, tri, mask, split=False)
        return carry, acc + _dot(w.astype(BF16), v)

    def run(j0, count):
        base = pl.multiple_of((j0 - (count - 1)) * tk, tk)
        z = _dot_nt(q, k_ref[pl.ds(base, count * tk), :]) + bias
        z2 = jnp.concatenate([z[:, a * tk:(a + 1) * tk] for a in range(count)], axis=0)
        neg_abs = lax.bitcast_convert_type(
            lax.bitcast_convert_type(z2, jnp.uint32) | jnp.uint32(0x80000000), F32)
        sp = jnp.maximum(z2, 0.0) + jnp.log2(1.0 + jnp.exp2(neg_abs))
        suffix = _dot(sp.astype(BF16), tri[:, :tk])
        row_sum = jnp.sum(sp, axis=1, keepdims=True)
        carry = carry_ref[...]
        carries = [None] * count
        for a in reversed(range(count)):
            carries[a] = jnp.concatenate([carry] * (tk // HEAD_DIM), axis=1)
            carry = carry + row_sum[a * tq:(a + 1) * tq]
        carry_ref[...] = carry
        w = jnp.exp2((z2 - sp) - suffix - jnp.concatenate(carries, axis=0))
        w = jnp.concatenate([w[a * tq:(a + 1) * tq] for a in range(count)], axis=1).astype(BF16)
        acc_ref[...] += _dot(w, v_ref[pl.ds(base, count * tk), :])

    row = lax.broadcasted_iota(jnp.int32, (tq, tk), 0)
    col = lax.broadcasted_iota(jnp.int32, (tq, tk), 1)
    zero = jnp.zeros((tq, HEAD_DIM), F32)
    carry_ref[...], acc_ref[...] = tile(i, zero, zero, col < row)

    group = PROMPT_TILES_PER_BODY
    n_groups = lax.div(i, group)

    def body(n, c):
        run(i - 1 - group * n, group)
        return c

    lax.fori_loop(0, n_groups, body, 0)
    left = i - group * n_groups
    part = group // 2
    while part >= 1:
        @pl.when((left & part) != 0)
        def _(part=part, left=left):
            run((left & (2 * part - 1)) - 1, part)
        part //= 2
    o_ref[...] = acc_ref[...].astype(BF16)


def _attn_prompt(q_b, k_b, v_b, bias2, seq):
    n_heads = q_b.shape[1] // HEAD_DIM
    kv = pl.BlockSpec((seq, HEAD_DIM), lambda h, i: (0, h))
    return pl.pallas_call(
        _attn_prompt_kernel,
        grid=(n_heads, seq // ATTN_TQ),
        in_specs=[pl.BlockSpec(memory_space=pltpu.SMEM),
                  pl.BlockSpec((ATTN_TQ, HEAD_DIM), lambda h, i: (i, h)),
                  kv, kv,
                  pl.BlockSpec((ATTN_TK, ATTN_TK + HEAD_DIM), lambda h, i: (0, 0))],
        out_specs=pl.BlockSpec((ATTN_TQ, HEAD_DIM), lambda h, i: (i, h)),
        out_shape=jax.ShapeDtypeStruct((seq, n_heads * HEAD_DIM), BF16),
        scratch_shapes=[pltpu.VMEM((ATTN_TQ, HEAD_DIM), F32), pltpu.VMEM((ATTN_TQ, HEAD_DIM), F32)],
        compiler_params=_params("arbitrary", "arbitrary"), name="attn_prompt",
    )(bias2, q_b, k_b, v_b, _suffix_matrix(ATTN_TK))


def _attn_sample_kernel(pt_ref, q_ref, kn_ref, vn_ref, *rest, n_heads, n_q, pages_per_step):
    kc_refs = rest[:pages_per_step]
    vc_refs = rest[pages_per_step:2 * pages_per_step]
    bias_ref, tri_ref, o_ref, carry_ref, acc_ref = rest[2 * pages_per_step:]
    p = pl.program_id(1)
    page = kn_ref.shape[0]
    rows = n_heads * n_q
    tri = tri_ref[...]
    bias = bias_ref[...]

    def head_q(hh):
        return q_ref[:, hh * HEAD_DIM:(hh + 1) * HEAD_DIM]

    def sweep(k_of, v_of, mask, carry, acc):
        z = jnp.concatenate([_dot_nt(head_q(hh), k_of(hh)) for hh in range(n_heads)], axis=0)
        w, carry = _stick_weights(z + bias, carry, tri, mask, split=True)
        return carry, acc + jnp.concatenate(
            [_dot(w[hh * n_q:(hh + 1) * n_q].astype(BF16), v_of(hh)) for hh in range(n_heads)], axis=0)

    @pl.when(p == 0)
    def _():
        q_idx = lax.broadcasted_iota(jnp.int32, (rows, page), 0) & (n_q - 1)
        key = lax.broadcasted_iota(jnp.int32, (rows, page), 1)
        zero = jnp.zeros((rows, HEAD_DIM), F32)
        carry_ref[...], acc_ref[...] = sweep(
            lambda hh: kn_ref[:, hh * HEAD_DIM:(hh + 1) * HEAD_DIM],
            lambda hh: vn_ref[:, hh * HEAD_DIM:(hh + 1) * HEAD_DIM], key < q_idx, zero, zero)

    def head_rows(ref, hh):
        return ref[pl.ds(hh, page, stride=n_heads), :].astype(BF16)

    pps = pages_per_step
    z = jnp.concatenate(
        [_dot_nt(head_q(hh), jnp.concatenate([head_rows(r, hh) for r in kc_refs], axis=0))
         for hh in range(n_heads)], axis=0)
    z2 = jnp.concatenate([z[:, r * page:(r + 1) * page] + bias for r in range(pps)], axis=0)
    neg_abs = lax.bitcast_convert_type(
        lax.bitcast_convert_type(z2, jnp.uint32) | jnp.uint32(0x80000000), F32)
    sp = jnp.maximum(z2, 0.0) + jnp.log2(1.0 + jnp.exp2(neg_abs))
    hi = sp.astype(BF16)
    tot = _dot(hi, tri) + _dot((sp - hi.astype(F32)).astype(BF16), tri)
    carry = carry_ref[...]
    carries = []
    for r in range(pps):
        carries.append(carry)
        carry = carry + tot[r * rows:(r + 1) * rows, page:]
    carry_ref[...] = carry
    w = jnp.exp2((z2 - sp) - tot[:, :page] - jnp.concatenate(carries, axis=0))
    acc_ref[...] += jnp.concatenate(
        [_dot(jnp.concatenate([w[r * rows + hh * n_q:r * rows + (hh + 1) * n_q] for r in range(pps)],
                              axis=1).astype(BF16),
              jnp.concatenate([head_rows(r, hh) for r in vc_refs], axis=0))
         for hh in range(n_heads)], axis=0)

    @pl.when(p == pl.num_programs(1) - 1)
    def _():
        acc = acc_ref[...]
        for hh in range(n_heads):
            o_ref[:, hh * HEAD_DIM:(hh + 1) * HEAD_DIM] = acc[hh * n_q:(hh + 1) * n_q]


SAMPLE_PAGES_PER_STEP = 16


def _attn_sample(q_s, k_new, v_new, cache_k, cache_v, layer, page_table, bias2):
    n_seq, n_q, width = q_s.shape
    n_heads = width // HEAD_DIM
    n_layers, n_pool, page = cache_k.shape[:3]
    n_pages = page_table.shape[1]
    pps = SAMPLE_PAGES_PER_STEP
    assert n_q & (n_q - 1) == 0 and n_pages % pps == 0
    rows = n_heads * n_q
    bias_rows = jnp.repeat(bias2, n_q).reshape(rows, 1)
    flat = (n_layers, n_pool, page * n_heads, HEAD_DIM)

    def cache(r):
        return pl.BlockSpec((None, None, page * n_heads, HEAD_DIM),
                            lambda b, p, pt: (layer, pt[b * n_pages + (n_pages - 1 - (p * pps + r))], 0, 0))

    new = pl.BlockSpec((None, page, width), lambda b, p, pt: (b, 0, 0))
    caches = [cache(r) for r in range(pps)]
    return pl.pallas_call(
        functools.partial(_attn_sample_kernel, n_heads=n_heads, n_q=n_q, pages_per_step=pps),
        grid_spec=pltpu.PrefetchScalarGridSpec(
            num_scalar_prefetch=1,
            grid=(n_seq, n_pages // pps),
            in_specs=[pl.BlockSpec((None, n_q, width), lambda b, p, pt: (b, 0, 0)),
                      new, new, *caches, *caches,
                      pl.BlockSpec((rows, 1), lambda b, p, pt: (0, 0)),
                      pl.BlockSpec((page, page + HEAD_DIM), lambda b, p, pt: (0, 0))],
            out_specs=pl.BlockSpec((None, n_q, width), lambda b, p, pt: (b, 0, 0)),
            scratch_shapes=[pltpu.VMEM((rows, HEAD_DIM), F32), pltpu.VMEM((rows, HEAD_DIM), F32)]),
        out_shape=jax.ShapeDtypeStruct((n_seq, n_q, width), F32),
        compiler_params=_params("arbitrary", "arbitrary"), name="attn_sample",
    )(page_table.reshape(-1), q_s, k_new, v_new,
      *([cache_k.reshape(flat)] * pps), *([cache_v.reshape(flat)] * pps), bias_rows, _suffix_matrix(page))


def _merge_kernel(a_ref, b_ref, ga_ref, gb_ref, wa_ref, wb_ref, o_ref):
    m = (ga_ref[...].astype(F32) * _dot(a_ref[...], wa_ref[...])
         + gb_ref[...].astype(F32) * _dot(b_ref[...], wb_ref[...]))
    o_ref[...] = m.astype(BF16)


def _merge(a, b, gates, wa_b, wb_b, layer):
    n, c = a.shape
    d = wa_b.shape[2]
    row = pl.BlockSpec((ROW_TILE, c), lambda i: (i, 0))
    w = pl.BlockSpec((None, c, d), lambda i: (layer, 0, 0))
    return pl.pallas_call(
        _merge_kernel, grid=(n // ROW_TILE,),
        in_specs=[row, row,
                  pl.BlockSpec((ROW_TILE, d), lambda i: (i, 0)),
                  pl.BlockSpec((ROW_TILE, d), lambda i: (i, 1)), w, w],
        out_specs=pl.BlockSpec((ROW_TILE, d), lambda i: (i, 0)),
        out_shape=jax.ShapeDtypeStruct((n, d), BF16),
        compiler_params=_params("arbitrary"), name="merge",
    )(a, b, gates, gates, wa_b, wb_b)


def _route(logits):
    lane_i = lax.broadcasted_iota(jnp.int32, logits.shape, 1)
    lane = lane_i.astype(F32)
    neg = jnp.float32(-jnp.inf)
    big = jnp.float32(ROUTER_LANES)
    is_group = (lane_i >= N_EXPERTS) & (lane_i < N_EXPERTS + MOE_GROUPS)
    g_log = jnp.where(is_group, logits, neg)
    g_max = jnp.max(g_log, axis=-1, keepdims=True)
    g_idx = jnp.min(jnp.where(g_log == g_max, lane, big), axis=-1, keepdims=True) - N_EXPERTS
    g_prob = 1.0 / jnp.sum(jnp.where(is_group, jnp.exp(g_log - g_max), 0.0), axis=-1, keepdims=True)
    lane_group = lax.shift_right_logical(lane_i, 2).astype(F32)
    in_group = (lane_i < N_EXPERTS) & (lane_group == g_idx)
    assert EXPERTS_PER_GROUP == 4
    e_log = jnp.where(in_group, logits, neg)
    v1 = jnp.max(e_log, axis=-1, keepdims=True)
    i1 = jnp.min(jnp.where(e_log == v1, lane, big), axis=-1, keepdims=True)
    e_log2 = jnp.where(lane == i1, neg, e_log)
    v2 = jnp.max(e_log2, axis=-1, keepdims=True)
    i2 = jnp.min(jnp.where(e_log2 == v2, lane, big), axis=-1, keepdims=True)
    e2 = jnp.exp(v2 - v1)
    den = 1.0 + e2
    w1 = g_prob / den
    w2 = g_prob * e2 / den
    gates = jnp.where(lane == i1, w1, jnp.where(lane == i2, w2, 0.0))
    return jnp.where(lane_i == GROUP_LANE, g_idx, gates)


def _out_kernel(m_ref, x_ref, w_ref, g1_ref, gn_ref, sc_ref, sh_ref, wr_hi_ref, wr_lo_ref, br_ref,
                x1_ref, h2_ref, gates_ref):
    x = x_ref[...]
    y3, g13 = _rows_mod(_dot(m_ref[...], w_ref[...]), g1_ref[...])
    x1 = x + (y3 * g13).reshape(x.shape)
    x1_ref[...] = x1
    h2 = _modulated_norm(x1, gn_ref[...], sc_ref[...], sh_ref[...])
    hi = h2.astype(BF16)
    d = x.shape[1]
    h2_ref[:, :d] = h2
    lo = (h2 - hi.astype(F32)).astype(BF16)
    wr_hi = wr_hi_ref[...]
    logits = _dot(hi, wr_hi) + (_dot(lo, wr_hi) + _dot(hi, wr_lo_ref[...]))
    routing = _route(logits + br_ref[...])
    gates_ref[...] = routing
    h2_ref[:, d:] = routing


def _out_proj(merged, x, w_out_b, layer, mod3, n_prompt_tiles, norm_g, w_router, b_router):
    n, d = x.shape
    w_router_hi = w_router.astype(BF16)
    row = pl.BlockSpec((ROW_TILE, d), lambda i: (i, 0))
    small = pl.BlockSpec((ROW_TILE, ROUTER_LANES), lambda i: (i, 0))
    return pl.pallas_call(
        _out_kernel, grid=(n // ROW_TILE,),
        in_specs=[row, row,
                  pl.BlockSpec((None, d, d), lambda i: (layer, 0, 0)),
                  _mod_spec(n_prompt_tiles, d, 2),
                  pl.BlockSpec((1, d), lambda i: (0, 0)),
                  _mod_spec(n_prompt_tiles, d, 4),
                  _mod_spec(n_prompt_tiles, d, 3),
                  pl.BlockSpec((d, ROUTER_LANES), lambda i: (0, 0)),
                  pl.BlockSpec((d, ROUTER_LANES), lambda i: (0, 0)),
                  pl.BlockSpec((1, ROUTER_LANES), lambda i: (0, 0))],
        out_specs=[row, pl.BlockSpec((ROW_TILE, d + ROUTER_LANES), lambda i: (i, 0)), small],
        out_shape=[jax.ShapeDtypeStruct((n, d), F32), jax.ShapeDtypeStruct((n, d + ROUTER_LANES), F32),
                   jax.ShapeDtypeStruct((n, ROUTER_LANES), F32)],
        compiler_params=_params("arbitrary"), name="out_proj",
    )(merged, x, w_out_b, mod3, norm_g.reshape(1, d), mod3, mod3, w_router_hi,
      (w_router - w_router_hi.astype(F32)).astype(BF16), b_router)


def _moe_kernel(tg_ref, src_ref, nact_ref, wg_ref, wu_ref, wd_ref, h_hbm, o_hbm, xbuf, ybuf, gsem, ssem, *, n_rows):
    t = pl.program_id(0)
    tm = xbuf.shape[1]
    d = ybuf.shape[2]
    slot = t & 1
    n_active = nact_ref[0]

    def gather(tile, row, r):
        return pltpu.make_async_copy(h_hbm.at[pl.ds(row, 1), :], xbuf.at[tile & 1, pl.ds(r, 1), :], gsem.at[tile & 1])

    def scatter(tile, row, r):
        return pltpu.make_async_copy(ybuf.at[tile & 1, pl.ds(r, 1), :], o_hbm.at[pl.ds(row, 1), :], ssem.at[tile & 1])

    def start_rows(tile, copy, clamp):
        def body(r, c):
            row = src_ref[tile * tm + r]
            copy(tile, jnp.minimum(row, n_rows - 1) if clamp else row, r).start()
            return c
        lax.fori_loop(0, tm, body, 0, unroll=8)

    def wait_rows(tile, copy):
        for r in range(tm):
            copy(tile, 0, r).wait()

    @pl.when(t == 0)
    def _():
        start_rows(0, gather, True)

    @pl.when(t + 1 < n_active)
    def _():
        start_rows(t + 1, gather, True)

    @pl.when(t < n_active)
    def _():
        wait_rows(t, gather)

        @pl.when(t >= 2)
        def _():
            wait_rows(t - 2, scatter)

        xb = xbuf[slot]
        x = xb[:, :d].astype(BF16)
        gates = xb[:, d:]
        lane = lax.broadcasted_iota(jnp.int32, gates.shape, 1)
        first = tg_ref[t] * EXPERTS_PER_GROUP
        acc = None
        for j in range(EXPERTS_PER_GROUP):
            gate = jnp.sum(jnp.where(lane == first + j, gates, 0.0), axis=-1, keepdims=True)
            hg = _dot(x, wg_ref[j])
            hu = _dot(x, wu_ref[j])
            act = (hg * jax.nn.sigmoid(hg)) * hu * gate
            contrib = _dot(act.astype(BF16), wd_ref[j])
            acc = contrib if acc is None else acc + contrib
        ybuf[slot] = acc
        start_rows(t, scatter, False)

    @pl.when(t == pl.num_programs(0) - 1)
    def _():
        last = n_active - 1
        wait_rows(last, scatter)

        @pl.when(last >= 1)
        def _():
            wait_rows(last - 1, scatter)


def _moe(h2r, route, wg_b, wu_b, wd_b, layer):
    n = h2r.shape[0]
    d = h2r.shape[1] - ROUTER_LANES
    f = wg_b.shape[3]
    tm = MOE_TILE
    n_tiles = n // tm + MOE_GROUPS
    n_sorted = n_tiles * tm
    grp = route[:, GROUP_LANE].astype(jnp.int32)
    onehot = grp[:, None] == jnp.arange(MOE_GROUPS, dtype=jnp.int32)[None, :]
    rank = jnp.sum(jnp.where(onehot, jnp.cumsum(onehot.astype(jnp.int32), axis=0) - 1, 0), axis=1)
    tiles_per = (jnp.sum(onehot.astype(jnp.int32), axis=0) + tm - 1) // tm
    tile_end = jnp.cumsum(tiles_per)
    dest = (tile_end - tiles_per)[grp] * tm + rank
    spare = n + jnp.arange(n_sorted, dtype=jnp.int32) % tm
    src = spare.at[dest].set(jnp.arange(n, dtype=jnp.int32))
    tile_group = jnp.minimum(jnp.sum(jnp.arange(n_tiles, dtype=jnp.int32)[:, None] >= tile_end[None, :], axis=1),
                             MOE_GROUPS - 1).astype(jnp.int32)

    def wspec(r, c):
        return pl.BlockSpec((None, EXPERTS_PER_GROUP, r, c), lambda t, tg, *_: (layer, tg[t], 0, 0))

    return pl.pallas_call(
        functools.partial(_moe_kernel, n_rows=n),
        grid_spec=pltpu.PrefetchScalarGridSpec(
            num_scalar_prefetch=3,
            grid=(n_tiles,),
            in_specs=[wspec(d, f), wspec(d, f), wspec(f, d), pl.BlockSpec(memory_space=pl.ANY)],
            out_specs=pl.BlockSpec(memory_space=pl.ANY),
            scratch_shapes=[pltpu.VMEM((2, tm, d + ROUTER_LANES), F32), pltpu.VMEM((2, tm, d), F32),
                            pltpu.SemaphoreType.DMA((2,)), pltpu.SemaphoreType.DMA((2,))]),
        out_shape=jax.ShapeDtypeStruct((n + tm, d), F32),
        compiler_params=_params("arbitrary"), name="moe",
    )(tile_group, src, tile_end[-1:], wg_b, wu_b, wd_b, h2r)


def _residual_kernel(x_ref, f_ref, g2_ref, op_ref, os_ref, *, n_prompt_tiles):
    f3, g23 = _rows_mod(f_ref[...], g2_ref[...])
    x = x_ref[...] + (f3 * g23).reshape(x_ref.shape)

    def store(ref):
        ref[...] = x

    _store_split(n_prompt_tiles, op_ref, os_ref, store)


def _residual(x1, ffn, mod3, n_prompt_tiles):
    n, d = x1.shape
    seq = n_prompt_tiles * ROW_TILE
    row = pl.BlockSpec((ROW_TILE, d), lambda i: (i, 0))
    return pl.pallas_call(
        functools.partial(_residual_kernel, n_prompt_tiles=n_prompt_tiles), grid=(n // ROW_TILE,),
        in_specs=[row, row, _mod_spec(n_prompt_tiles, d, 5)],
        out_specs=_split_specs(n_prompt_tiles, (ROW_TILE, d)),
        out_shape=[jax.ShapeDtypeStruct((seq, d), F32), jax.ShapeDtypeStruct((n - seq, d), F32)],
        compiler_params=_params("arbitrary"), name="residual",
    )(x1, ffn, mod3)


def kernel(x_prompt, x_sample, c_prompt, c_sample, cache_k, cache_v, state_ssm_re, state_ssm_im, page_table, w_ada, b_ada, norm_mix_g, w_in, ssm_a_re, ssm_a_im, ssm_log_dt, ssm_b_re, ssm_b_im, ssm_c_re, ssm_c_im, ssm_d, w_glu, b_glu, q_norm_g, k_norm_g, sb_bias, w_branch_ssm, w_branch_attn, w_out, norm_ffn_g, w_router_group, b_router_group, w_router_expert, b_router_expert, w_exp_gate, w_exp_up, w_exp_down):
    n_layers = w_ada.shape[0]
    bsz, seq, d = x_prompt.shape
    n_seq, n_q, _ = x_sample.shape
    assert bsz == 1 and n_q == SUBLANES and n_seq * n_q == ROW_TILE and seq % ROW_TILE == 0
    n_s = n_seq * n_q
    n_prompt_tiles = seq // ROW_TILE
    mod_rows = ROW_TILE // SUBLANES
    ssm_w = ssm_d.shape[1]
    attn_w = w_branch_attn.shape[1]
    assert ssm_w == attn_w
    n_heads = attn_w // HEAD_DIM
    n_groups = ssm_w // SSM_GROUP
    page = cache_k.shape[2]
    chunk_p, chunk_s = 16, n_q
    assert seq % chunk_p == 0 and chunk_p == 2 * chunk_s and page == HEAD_DIM
    chunks = (chunk_p, chunk_s)

    w_in_b = w_in.astype(BF16)
    w_glu_b = w_glu.astype(BF16)
    w_bs_b = w_branch_ssm.astype(BF16)
    w_ba_b = w_branch_attn.astype(BF16)
    w_out_b = w_out.astype(BF16)
    wg_b, wu_b, wd_b = w_exp_gate.astype(BF16), w_exp_up.astype(BF16), w_exp_down.astype(BF16)

    c_all = jnp.concatenate([c_prompt, c_sample], axis=0)
    pad = -c_all.shape[0] % SUBLANES
    mod = _adaln(jnp.pad(c_all, ((0, pad), (0, 0))), w_ada, b_ada)

    x = jnp.concatenate([x_prompt.reshape(seq, d), x_sample.reshape(n_s, d)], axis=0)
    outs = {k: [] for k in ("rp", "ip", "rs", "is")}
    kv_out = tuple(jnp.zeros((n_layers, rows * n_heads, HEAD_DIM), F32) for rows in (seq, n_s, seq, n_s))
    for l in range(n_layers):
        mod3_prev = mod3 if l else None
        mod3 = jnp.stack([jnp.broadcast_to(mod[l, 0], (mod_rows, N_MOD * d)), mod[l, 1:1 + n_seq]])
        if l == 0:
            h = _norm_mod(x, norm_mix_g[l], mod3, n_prompt_tiles, 1, 0)
        else:
            x, h = _res_norm_mod(x1, ffn, mod3_prev, norm_mix_g[l], mod3, n_prompt_tiles)
        u, up, us, q_b, k_b, v_b, gates, kv_out = _in_proj(h, w_in_b, l, q_norm_g[l], k_norm_g[l], ssm_w,
                                                           n_prompt_tiles, kv_out, chunks)

        ssm_par = (ssm_a_re[l], ssm_a_im[l], ssm_log_dt[l], ssm_b_re[l], ssm_b_im[l], ssm_c_re[l], ssm_c_im[l])
        prep_p = _ssm_prep(*ssm_par, chunk_p)
        loc_re, loc_im = _ssm_loc(up, prep_p[1], prep_p[2])
        sp_re, sp_im = _ssm_scan(loc_re, loc_im, prep_p[5], prep_p[6])
        yp, np_re, np_im = _ssm_out(up, sp_re, sp_im, prep_p, False)
        h0_re = state_ssm_re[l].reshape(n_seq, n_groups * SSM_STATE)
        h0_im = state_ssm_im[l].reshape(n_seq, n_groups * SSM_STATE)
        ys, ns_re, ns_im = _ssm_out(us, h0_re, h0_im, prep_p, True)
        a_branch = _glu(yp, ys, u, ssm_d[l], w_glu_b, l, b_glu[l], n_prompt_tiles, chunks)

        bias2 = sb_bias[l].astype(F32) * LOG2_E
        bp = _attn_prompt(q_b, k_b, v_b, bias2, seq)
        new_pad = ((0, 0), (0, page - n_q), (0, 0))
        bs = _attn_sample(q_b[seq:].reshape(n_seq, n_q, attn_w),
                          jnp.pad(k_b[seq:].reshape(n_seq, n_q, attn_w), new_pad),
                          jnp.pad(v_b[seq:].reshape(n_seq, n_q, attn_w), new_pad),
                          cache_k, cache_v, l, page_table, bias2)
        b_branch = jnp.concatenate([bp, bs.reshape(n_s, attn_w).astype(BF16)], axis=0)

        merged = _merge(a_branch, b_branch, gates, w_bs_b, w_ba_b, l)
        w_router = jnp.pad(jnp.concatenate([w_router_expert[l], w_router_group[l]], axis=1),
                           ((0, 0), (0, ROUTER_LANES - N_EXPERTS - MOE_GROUPS)))
        b_router = jnp.pad(jnp.concatenate([b_router_expert[l], b_router_group[l]]),
                           (0, ROUTER_LANES - N_EXPERTS - MOE_GROUPS)).reshape(1, ROUTER_LANES)
        x1, h2, route = _out_proj(merged, x, w_out_b, l, mod3, n_prompt_tiles, norm_ffn_g[l], w_router, b_router)
        ffn = _moe(h2, route, wg_b, wu_b, wd_b, l)

        outs["rp"].append(np_re[-1].reshape(1, n_groups, SSM_STATE))
        outs["ip"].append(np_im[-1].reshape(1, n_groups, SSM_STATE))
        outs["rs"].append(ns_re.reshape(n_seq, n_groups, SSM_STATE))
        outs["is"].append(ns_im.reshape(n_seq, n_groups, SSM_STATE))

    y_p, y_s = _residual(x1, ffn, mod3, n_prompt_tiles)
    st = lambda k: jnp.stack(outs[k])
    kp, ks, vp, vs = kv_out
    return (y_p.reshape(1, seq, d), y_s.reshape(n_seq, n_q, d),
            kp.reshape(n_layers, 1, seq, n_heads, HEAD_DIM), vp.reshape(n_layers, 1, seq, n_heads, HEAD_DIM),
            st("rp"), st("ip"),
            ks.reshape(n_layers, n_seq, n_q, n_heads, HEAD_DIM), vs.reshape(n_layers, n_seq, n_q, n_heads, HEAD_DIM),
            st("rs"), st("is"))
```
